```python
import functools
import jax, jax.numpy as jnp
from jax import lax
import numpy as np

D_MODEL = 1024
BATCH = 8
SEQ = 4096
DEPTH = 2

HEAD_DIM = 64
GROUP_WIDTH = D_MODEL // 4
MIX_WIDTH = 4 * GROUP_WIDTH
N_GROUP_HEADS = GROUP_WIDTH // HEAD_DIM
SHORT_CONV_K = 3
DSWA_CONFIGS = ((128, 1), (512, 4), (2048, 16))
POOL_WINDOWS = (2, 4, 8, 16)
POOL_GROUP = GROUP_WIDTH // len(POOL_WINDOWS)
ROPE_THETA = 500000.0
ROPE_DIM = HEAD_DIM // 4
Q_BLOCK = 128
MEM_LEN = 256
XA_HEADS = 4
XA_HEAD_DIM = D_MODEL // XA_HEADS
D_FF = ((8 * D_MODEL // 3 + 127) // 128) * 128
FFN_CONV_K = 3
RMS_EPS = 1e-6
NEG_INF = -1e30
FORGET_BIAS_INIT = 3.0
N_IN = 3 * GROUP_WIDTH + 3 * GROUP_WIDTH + 3 * GROUP_WIDTH + N_GROUP_HEADS + GROUP_WIDTH

kernel_name = "hybrid_parallel_heads_conv_dilated_fox_pool"


def rms_norm(t, g):
    tf = t.astype(jnp.float32)
    y = tf * lax.rsqrt(jnp.mean(tf * tf, axis=-1, keepdims=True) + RMS_EPS)
    return (y * g.astype(jnp.float32)).astype(t.dtype)


def causal_depthwise_conv(t, w):
    K, C = w.shape
    return lax.conv_general_dilated(
        t, w.astype(t.dtype)[:, None, :], window_strides=(1,), padding=((K - 1, 0),),
        dimension_numbers=("NWC", "WIO", "NWC"), feature_group_count=C)


def to_heads(t):
    B, S, W = t.shape
    return t.reshape(B, S, W // HEAD_DIM, HEAD_DIM).transpose(0, 2, 1, 3)


def from_heads(t):
    B, H, S, dh = t.shape
    return t.transpose(0, 2, 1, 3).reshape(B, S, H * dh)


def rope_tables(positions):
    inv_freq = ROPE_THETA ** (-jnp.arange(0, ROPE_DIM, 2, dtype=jnp.float32) / ROPE_DIM)
    ang = positions.astype(jnp.float32)[:, None, :, None] * inv_freq
    return jnp.cos(ang), jnp.sin(ang)


def apply_partial_rope(t, cos, sin):
    tf = t.astype(jnp.float32)
    r1, r2 = jnp.split(tf[..., :ROPE_DIM], 2, axis=-1)
    rot = jnp.concatenate([r1 * cos - r2 * sin, r2 * cos + r1 * sin], axis=-1)
    return jnp.concatenate([rot, tf[..., ROPE_DIM:]], axis=-1).astype(t.dtype)


def short_conv_mixer(p, w_conv):
    h, b_gate, c_gate = jnp.split(p, 3, axis=-1)
    return b_gate * causal_depthwise_conv(c_gate * h, w_conv)


def dilated_attention(q, k, v):
    B, H, S, dh = q.shape
    nb = S // Q_BLOCK
    qb = q.reshape(B, H, nb, Q_BLOCK, dh).transpose(2, 0, 1, 3, 4)
    kf = k.astype(jnp.float32)
    vf = v.astype(jnp.float32)
    scale = dh ** -0.5

    def block(args):
        i, qi = args
        t = i * Q_BLOCK + jnp.arange(Q_BLOCK)
        qs = qi.astype(jnp.float32) * scale
        scores, values = [], []
        for window, dil in DSWA_CONFIGS:
            idx = t[:, None] - dil * jnp.arange(window // dil + 1)[None, :]
            idxc = jnp.maximum(idx, 0)
            kg = kf[:, :, idxc]
            s = jnp.einsum("bhqd,bhqnd->bhqn", qs, kg)
            scores.append(jnp.where(idx >= 0, s, NEG_INF))
            values.append(vf[:, :, idxc])
        m = functools.reduce(jnp.maximum, [s.max(axis=-1, keepdims=True) for s in scores])
        num, den = 0.0, 0.0
        for s, vg in zip(scores, values):
            e = jnp.exp(s - m)
            num = num + jnp.einsum("bhqn,bhqnd->bhqd", e, vg)
            den = den + e.sum(axis=-1, keepdims=True)
        return num / den

    o = lax.map(block, (jnp.arange(nb), qb))
    return o.transpose(1, 2, 0, 3, 4).reshape(B, H, S, dh).astype(q.dtype)


def dilated_mixer(p, cos, sin):
    q, k, v = [to_heads(t) for t in jnp.split(p, 3, axis=-1)]
    q = apply_partial_rope(q, cos, sin)
    k = apply_partial_rope(k, cos, sin)
    return from_heads(dilated_attention(q, k, v))


def forgetting_attention(q, k, v, c):
    B, H, S, dh = q.shape
    nb = S // Q_BLOCK
    qb = q.reshape(B, H, nb, Q_BLOCK, dh).transpose(2, 0, 1, 3, 4)
    cb = c.reshape(B, H, nb, Q_BLOCK).transpose(2, 0, 1, 3)
    kf = k.astype(jnp.float32)
    vf = v.astype(jnp.float32)
    kpos = jnp.arange(S)
    scale = dh ** -0.5

    def block(args):
        i, qi, ci = args
        t = i * Q_BLOCK + jnp.arange(Q_BLOCK)
        s = jnp.einsum("bhqd,bhkd->bhqk", qi.astype(jnp.float32) * scale, kf)
        s = s + ci[..., None] - c[:, :, None, :]
        s = jnp.where(kpos[None, :] <= t[:, None], s, NEG_INF)
        return jnp.einsum("bhqk,bhkd->bhqd", jax.nn.softmax(s, axis=-1), vf)

    o = lax.map(block, (jnp.arange(nb), qb, cb))
    return o.transpose(1, 2, 0, 3, 4).reshape(B, H, S, dh).astype(q.dtype)


def forgetting_mixer(p, b_f):
    G = GROUP_WIDTH
    q, k, v = to_heads(p[..., :G]), to_heads(p[..., G:2 * G]), to_heads(p[..., 2 * G:3 * G])
    logf = jax.nn.log_sigmoid(p[..., 3 * G:].astype(jnp.float32) + b_f.astype(jnp.float32))
    c = jnp.cumsum(logf, axis=1).transpose(0, 2, 1)
    return from_heads(forgetting_attention(q, k, v, c))


def pool_mixer(p, w_pool, scale):
    B, S, _ = p.shape
    pf = p.astype(jnp.float32).reshape(B, S, len(POOL_WINDOWS), POOL_GROUP)
    cs = jnp.cumsum(pf, axis=1)
    cs0 = jnp.concatenate([jnp.zeros_like(cs[:, :1]), cs], axis=1)
    t = jnp.arange(S)
    outs = []
    for g, w in enumerate(POOL_WINDOWS):
        upper = cs0[:, 1:, g]
        lower = cs0[:, jnp.maximum(t + 1 - w, 0), g]
        cnt = jnp.minimum(t + 1, w).astype(jnp.float32)[None, :, None]
        outs.append((upper - lower) / cnt - pf[:, :, g])
    z = jnp.stack(outs, axis=2)
    y = jnp.einsum("bsgc,gcd->bsgd", z, w_pool.astype(jnp.float32)).reshape(B, S, GROUP_WIDTH)
    return (y * scale.astype(jnp.float32)).astype(p.dtype)


def memory_cross_attention(xn, memn, w_q, w_kv, w_o):
    B, S, D = xn.shape
    M = memn.shape[1]
    q = (xn @ w_q).reshape(B, S, XA_HEADS, XA_HEAD_DIM)
    k, v = jnp.split(memn @ w_kv, 2, axis=-1)
    k = k.reshape(B, M, XA_HEADS, XA_HEAD_DIM)
    v = v.reshape(B, M, XA_HEADS, XA_HEAD_DIM)
    s = jnp.einsum("bshd,bmhd->bhsm", q.astype(jnp.float32), k.astype(jnp.float32)) * XA_HEAD_DIM ** -0.5
    o = jnp.einsum("bhsm,bmhd->bshd", jax.nn.softmax(s, axis=-1), v.astype(jnp.float32))
    return o.reshape(B, S, D).astype(xn.dtype) @ w_o


def conv_ffn(xn, w_up, w_conv, w_down):
    u = causal_depthwise_conv(xn @ w_up, w_conv)
    a, g = jnp.split(u, 2, axis=-1)
    return (a * jax.nn.silu(g)) @ w_down


def setup_inputs(seed: int = 0) -> dict:
    key = jax.random.key(seed)
    ks = jax.random.split(key, 24)
    f32 = jnp.float32
    nrm = lambda k, shape, s: jax.random.normal(k, shape, f32) * s
    L, D, G = DEPTH, D_MODEL, GROUP_WIDTH
    x = jax.random.normal(ks[0], (BATCH, SEQ, D), f32)
    mem = jax.random.normal(ks[1], (BATCH, MEM_LEN, D), f32)
    offset = jax.random.randint(ks[2], (BATCH, 1), 0, 1024, dtype=jnp.int32)
    positions = (jnp.arange(SEQ, dtype=jnp.int32)[None, :] + offset).astype(jnp.int32)
    return {
        "x": x,
        "mem": mem,
        "positions": positions,
        "g_mix": 1.0 + nrm(ks[3], (L, D), 0.02),
        "w_in": nrm(ks[4], (L, D, N_IN), D ** -0.5),
        "b_forget": FORGET_BIAS_INIT + nrm(ks[5], (L, N_GROUP_HEADS), 0.1),
        "w_sconv": nrm(ks[6], (L, SHORT_CONV_K, G), SHORT_CONV_K ** -0.5),
        "w_pool": nrm(ks[7], (L, len(POOL_WINDOWS), POOL_GROUP, POOL_GROUP), POOL_GROUP ** -0.5),
        "pool_scale": 1.0 + nrm(ks[8], (L, G), 0.02),
        "w_out": nrm(ks[9], (L, MIX_WIDTH, D), MIX_WIDTH ** -0.5),
        "g_xa": 1.0 + nrm(ks[10], (L, D), 0.02),
        "g_mem": 1.0 + nrm(ks[11], (L, D), 0.02),
        "w_xq": nrm(ks[12], (L, D, D), D ** -0.5),
        "w_xkv": nrm(ks[13], (L, D, 2 * D), D ** -0.5),
        "w_xo": nrm(ks[14], (L, D, D), D ** -0.5),
        "g_ffn": 1.0 + nrm(ks[15], (L, D), 0.02),
        "w_up": nrm(ks[16], (L, D, 2 * D_FF), D ** -0.5),
        "w_ffconv": nrm(ks[17], (L, FFN_CONV_K, 2 * D_FF), FFN_CONV_K ** -0.5),
        "w_down": nrm(ks[18], (L, D_FF, D), D_FF ** -0.5),
        "g_final": 1.0 + nrm(ks[19], (D,), 0.02),
    }


def reference(x, mem, positions, g_mix, w_in, b_forget, w_sconv, w_pool, pool_scale, w_out,
              g_xa, g_mem, w_xq, w_xkv, w_xo, g_ffn, w_up, w_ffconv, w_down, g_final):
    G = GROUP_WIDTH
    cos, sin = rope_tables(positions)
    h = x
    for l in range(DEPTH):
        xn = rms_norm(h, g_mix[l])
        proj = xn @ w_in[l]
        pa, pb, pc, pd = jnp.split(proj, [3 * G, 6 * G, 9 * G + N_GROUP_HEADS], axis=-1)
        ya = short_conv_mixer(pa, w_sconv[l])
        yb = dilated_mixer(pb, cos, sin)
        yc = forgetting_mixer(pc, b_forget[l])
        yd = pool_mixer(pd, w_pool[l], pool_scale[l])
        h = h + jnp.concatenate([ya, yb, yc, yd], axis=-1) @ w_out[l]
        h = h + memory_cross_attention(rms_norm(h, g_xa[l]), rms_norm(mem, g_mem[l]),
                                       w_xq[l], w_xkv[l], w_xo[l])
        h = h + conv_ffn(rms_norm(h, g_ffn[l]), w_up[l], w_ffconv[l], w_down[l])
    return rms_norm(h, g_final)
```

```python
import functools

import jax
import jax.numpy as jnp
from jax import lax
from jax.experimental import pallas as pl
from jax.experimental.pallas import tpu as pltpu

D_MODEL = 1024
HEAD_DIM = 64
GROUP_WIDTH = D_MODEL // 4
N_GROUP_HEADS = GROUP_WIDTH // HEAD_DIM
DSWA_CONFIGS = ((128, 1), (512, 4), (2048, 16))
POOL_WINDOWS = (2, 4, 8, 16)
POOL_GROUP = GROUP_WIDTH // len(POOL_WINDOWS)
ROPE_THETA = 500000.0
ROPE_DIM = HEAD_DIM // 4
XA_HEADS = 4
XA_HEAD_DIM = D_MODEL // XA_HEADS
D_FF = ((8 * D_MODEL // 3 + 127) // 128) * 128
RMS_EPS = 1e-6
NEG_INF = -1e30

LANES = 128
SUBLANES = 8
MXU_DIM = 256
VMEM_LIMIT_BYTES = 56 * 1024 * 1024

COL_CONV = 0
COL_DSWA = 3 * GROUP_WIDTH
COL_FOX = 6 * GROUP_WIDTH
COL_POOL = 9 * GROUP_WIDTH
COL_FORGET = 10 * GROUP_WIDTH
N_IN_PADDED = COL_FORGET + LANES

TOKEN_TILE = 512
DSWA_BLOCK = 128
FOX_BLOCK = 512
FFN_CHUNK = MXU_DIM

_NT = (((1,), (1,)), ((), ()))


def _dot(a, b):
    return jnp.dot(a, b, preferred_element_type=jnp.float32)


def _dot_nt(a, b):
    return lax.dot_general(a, b, _NT, preferred_element_type=jnp.float32)


def _rms_norm(x, g):
    return x * lax.rsqrt(jnp.mean(x * x, axis=-1, keepdims=True) + RMS_EPS) * g


def _params(semantics):
    return pltpu.CompilerParams(dimension_semantics=semantics, vmem_limit_bytes=VMEM_LIMIT_BYTES)


def _const_spec(shape):
    zeros = (0,) * len(shape)
    return pl.BlockSpec(shape, lambda *_: zeros)


def _rope_table_kernel(pos_ref, consts_ref, cos_ref, sin_ref):
    ang = pos_ref[...].astype(jnp.float32) * consts_ref[0:1, :]
    cos_ref[...] = jnp.where(consts_ref[1:2, :] > 0.0, jnp.cos(ang), 1.0)
    sin_ref[...] = jnp.sin(ang) * consts_ref[2:3, :]


def _rope_tables(positions):
    n_tok = positions.size
    lane = jnp.arange(LANES) % HEAD_DIM
    inv_freq = ROPE_THETA ** (-jnp.arange(0, ROPE_DIM, 2, dtype=jnp.float32) / ROPE_DIM)
    half = ROPE_DIM // 2
    consts = jnp.zeros((SUBLANES, LANES), jnp.float32)
    consts = consts.at[0].set(inv_freq[lane % half])
    consts = consts.at[1].set((lane < ROPE_DIM).astype(jnp.float32))
    consts = consts.at[2].set(jnp.where(lane < half, -1.0, jnp.where(lane < ROPE_DIM, 1.0, 0.0)))
    tile = 2048
    table = jax.ShapeDtypeStruct((n_tok, LANES), jnp.float32)
    return pl.pallas_call(
        _rope_table_kernel,
        grid=(n_tok // tile,),
        in_specs=[pl.BlockSpec((tile, 1), lambda i: (i, 0)), _const_spec((SUBLANES, LANES))],
        out_specs=[pl.BlockSpec((tile, LANES), lambda i: (i, 0))] * 2,
        out_shape=[table, table],
        compiler_params=_params(("arbitrary",)),
        name="rope_tables",
    )(positions.reshape(n_tok, 1), consts)


def _shift_rows(x, k):
    return pltpu.roll(x, k, 0)


def _mix_in_kernel(h_ref, g_ref, w_ref, cos_ref, sin_ref, wsc_ref, wpool_ref, pscale_ref, bf_ref,
                   pwin_ref, ya_ref, qkvb_ref, qkvc_ref, yd_ref, c_ref, ct_ref,
                   halo_a, halo_d, carry_c, *, tiles_per_batch):
    tm = h_ref.shape[0]
    G = GROUP_WIDTH
    t = pl.program_id(0) % tiles_per_batch

    @pl.when(t == 0)
    def _():
        halo_a[...] = jnp.zeros_like(halo_a)
        halo_d[...] = jnp.zeros_like(halo_d)
        carry_c[...] = jnp.zeros_like(carry_c)

    xn = _rms_norm(h_ref[...], g_ref[...]).astype(jnp.bfloat16)

    pa = _dot(xn, w_ref[:, COL_CONV:COL_CONV + 3 * G])
    x = pa[:, 2 * G:3 * G] * pa[:, 0:G]
    ext = jnp.concatenate([halo_a[...], x], axis=0)
    conv = (wsc_ref[2:3, :] * ext + wsc_ref[1:2, :] * _shift_rows(ext, 1)
            + wsc_ref[0:1, :] * _shift_rows(ext, 2))
    ya_ref[...] = (pa[:, G:2 * G] * conv[SUBLANES:, :]).astype(ya_ref.dtype)
    halo_a[...] = x[tm - SUBLANES:, :]

    pd = _dot(xn, w_ref[:, COL_POOL:COL_POOL + G])
    halo_rows = halo_d.shape[0]
    extd = jnp.concatenate([halo_d[...], pd], axis=0)
    s2 = extd + _shift_rows(extd, 1)
    s4 = s2 + _shift_rows(s2, 2)
    s8 = s4 + _shift_rows(s4, 4)
    s16 = s8 + _shift_rows(s8, 8)
    lane = lax.broadcasted_iota(jnp.int32, (1, G), 1)
    wsum = jnp.where(lane < POOL_GROUP, s2,
                     jnp.where(lane < 2 * POOL_GROUP, s4,
                               jnp.where(lane < 3 * POOL_GROUP, s8, s16)))[halo_rows:, :]
    tpos = t * tm + lax.broadcasted_iota(jnp.int32, (tm, 1), 0)
    cnt = jnp.minimum(tpos + 1, pwin_ref[...]).astype(jnp.float32)
    z = wsum / cnt - pd
    yd = _dot(z.astype(jnp.bfloat16), wpool_ref[...]) * pscale_ref[...]
    yd_ref[...] = yd.astype(yd_ref.dtype)
    halo_d[...] = pd[tm - halo_rows:, :]

    pf = _dot(xn, w_ref[:, COL_FORGET:COL_FORGET + LANES]) + bf_ref[...]
    c = jnp.minimum(pf, 0.0) - jnp.log(1.0 + jnp.exp(-jnp.abs(pf)))
    row = lax.broadcasted_iota(jnp.int32, (tm, LANES), 0)
    k = 1
    while k < tm:
        c = c + jnp.where(row >= k, _shift_rows(c, k), 0.0)
        k *= 2
    c = c + carry_c[SUBLANES - 1:SUBLANES, :]
    c_ref[...] = c
    ct_ref[...] = c.T[0:SUBLANES, :]
    carry_c[...] = c[tm - SUBLANES:, :]

    cos2 = jnp.concatenate([cos_ref[...]] * (G // LANES), axis=1)
    sin2 = jnp.concatenate([sin_ref[...]] * (G // LANES), axis=1)
    first_half = (lane % HEAD_DIM) < (ROPE_DIM // 2)

    def rope(v):
        partner = jnp.where(first_half, pltpu.roll(v, G - ROPE_DIM // 2, 1),
                            pltpu.roll(v, ROPE_DIM // 2, 1))
        return v * cos2 + partner * sin2

    scale = HEAD_DIM ** -0.5
    pb = _dot(xn, w_ref[:, COL_DSWA:COL_DSWA + 3 * G])
    qkv_roped = (rope(pb[:, 0:G]) * scale, rope(pb[:, G:2 * G]), pb[:, 2 * G:3 * G])
    for part, val in enumerate(qkv_roped):
        for tile in range(G // LANES):
            qkvb_ref[part * (G // LANES) + tile] = val[:, tile * LANES:(tile + 1) * LANES]

    pc = _dot(xn, w_ref[:, COL_FOX:COL_FOX + 3 * G])
    qkvc_ref[:, 0:G] = (pc[:, 0:G] * scale).astype(qkvc_ref.dtype)
    qkvc_ref[:, G:3 * G] = pc[:, G:3 * G].astype(qkvc_ref.dtype)


def _mix_in(h, g, w_in_p, cos_t, sin_t, wsc, wpool_bd, pscale, bf, pwin, seq):
    n_tok = h.shape[0]
    tm = TOKEN_TILE
    G = GROUP_WIDTH
    row_spec = lambda width: pl.BlockSpec((tm, width), lambda i: (i, 0))
    bf16 = jnp.bfloat16
    return pl.pallas_call(
        functools.partial(_mix_in_kernel, tiles_per_batch=seq // tm),
        grid=(n_tok // tm,),
        in_specs=[row_spec(D_MODEL), _const_spec((1, D_MODEL)), _const_spec((D_MODEL, N_IN_PADDED)),
                  row_spec(LANES), row_spec(LANES), _const_spec((SUBLANES, G)),
                  _const_spec((G, G)), _const_spec((1, G)), _const_spec((1, LANES)),
                  _const_spec((1, G))],
        out_specs=[row_spec(G), pl.BlockSpec((3 * G // LANES, tm, LANES), lambda i: (0, i, 0)),
                   row_spec(3 * G), row_spec(G), row_spec(LANES),
                   pl.BlockSpec((SUBLANES, tm), lambda i: (0, i))],
        out_shape=[jax.ShapeDtypeStruct((n_tok, G), bf16),
                   jax.ShapeDtypeStruct((3 * G // LANES, n_tok, LANES), jnp.float32),
                   jax.ShapeDtypeStruct((n_tok, 3 * G), bf16),
                   jax.ShapeDtypeStruct((n_tok, G), bf16),
                   jax.ShapeDtypeStruct((n_tok, LANES), jnp.float32),
                   jax.ShapeDtypeStruct((SUBLANES, n_tok), jnp.float32)],
        scratch_shapes=[pltpu.VMEM((SUBLANES, G), jnp.float32),
                        pltpu.VMEM((2 * SUBLANES, G), jnp.float32),
                        pltpu.VMEM((SUBLANES, LANES), jnp.float32)],
        compiler_params=_params(("arbitrary",)),
        name="mix_in",
    )(h, g, w_in_p, cos_t, sin_t, wsc, wpool_bd, pscale, bf, pwin)


def _dswa_kernel(qkv_ref, out_ref, m_s, l_s, a_s):
    seq = qkv_ref.shape[1]
    QB = DSWA_BLOCK
    n_tiles = GROUP_WIDTH // LANES
    heads_per_tile = LANES // HEAD_DIM
    lane = lax.broadcasted_iota(jnp.int32, (1, LANES), 1)
    head_masks = [(lane // HEAD_DIM) == h for h in range(heads_per_tile)]
    qrow = lax.broadcasted_iota(jnp.int32, (QB, 2 * QB), 0)
    kcol = lax.broadcasted_iota(jnp.int32, (QB, 2 * QB), 1)
    band = (kcol >= qrow) & (kcol <= qrow + QB)

    for cfg, (window, dil) in enumerate(DSWA_CONFIGS):
        assert window == QB * dil
        nblk = seq // (QB * dil)

        def block(n, carry, cfg=cfg, dil=dil, nblk=nblk):
            rho = n // nblk
            i = n % nblk
            start = rho + (QB * dil) * i
            prev = rho + (QB * dil) * jnp.maximum(i - 1, 0)
            if dil > 1:
                rows, prows = pl.ds(start, QB, stride=dil), pl.ds(prev, QB, stride=dil)
            else:
                rows, prows = pl.ds(pl.multiple_of(start, QB), QB), pl.ds(pl.multiple_of(prev, QB), QB)
            valid = band & ((kcol >= QB) | (i > 0))
            for tile in range(n_tiles):
                q = qkv_ref[tile, rows, :]
                k = jnp.concatenate([qkv_ref[n_tiles + tile, prows, :],
                                     qkv_ref[n_tiles + tile, rows, :]], axis=0)
                v = jnp.concatenate([qkv_ref[2 * n_tiles + tile, prows, :],
                                     qkv_ref[2 * n_tiles + tile, rows, :]], axis=0)
                kb = k.astype(jnp.bfloat16)
                probs = []
                m_full = jnp.zeros((QB, LANES), jnp.float32)
                l_full = jnp.zeros((QB, LANES), jnp.float32)
                for hm in head_masks:
                    qh = jnp.where(hm, q, 0.0).astype(jnp.bfloat16)
                    s = jnp.where(valid, _dot_nt(qh, kb), NEG_INF)
                    m = jnp.max(s, axis=1, keepdims=True)
                    p = jnp.exp(s - m)
                    l = jnp.sum(p, axis=1, keepdims=True)
                    probs.append(p.astype(jnp.bfloat16))
                    m_full = jnp.where(hm, m, m_full)
                    l_full = jnp.where(hm, l, l_full)
                p_all = jnp.concatenate(probs, axis=1)
                v_bd = jnp.concatenate(
                    [jnp.where(hm, v, 0.0).astype(jnp.bfloat16) for hm in head_masks], axis=0)
                acc = _dot(p_all, v_bd)
                if cfg == 0:
                    m_s[tile, rows, :] = m_full
                    l_s[tile, rows, :] = l_full
                    a_s[tile, rows, :] = acc
                else:
                    m_old = m_s[tile, rows, :]
                    m_new = jnp.maximum(m_old, m_full)
                    w_old = jnp.exp(m_old - m_new)
                    w_cur = jnp.exp(m_full - m_new)
                    m_s[tile, rows, :] = m_new
                    l_s[tile, rows, :] = l_s[tile, rows, :] * w_old + l_full * w_cur
                    a_s[tile, rows, :] = a_s[tile, rows, :] * w_old + acc * w_cur
            return carry

        lax.fori_loop(0, seq // QB, block, 0)

    chunk = TOKEN_TILE
    for tile in range(n_tiles):
        for r in range(0, seq, chunk):
            out_ref[0, r:r + chunk, tile * LANES:(tile + 1) * LANES] = (
                a_s[tile, r:r + chunk, :] / l_s[tile, r:r + chunk, :]).astype(out_ref.dtype)


def _dswa(qkvb, batch, seq):
    G = GROUP_WIDTH
    n_slabs = 3 * G // LANES
    return pl.pallas_call(
        _dswa_kernel,
        grid=(batch,),
        in_specs=[pl.BlockSpec((n_slabs, seq, LANES), lambda b: (0, b, 0))],
        out_specs=pl.BlockSpec((1, seq, G), lambda b: (b, 0, 0)),
        out_shape=jax.ShapeDtypeStruct((batch, seq, G), jnp.bfloat16),
        scratch_shapes=[pltpu.VMEM((G // LANES, seq, LANES), jnp.float32)] * 3,
        compiler_params=_params(("arbitrary",)),
        name="dswa",
    )(qkvb)


def _fox_kernel(qkv_ref, c_ref, ct_ref, out_ref):
    G = GROUP_WIDTH
    BQ = FOX_BLOCK
    qi = pl.program_id(1)
    q0 = pl.multiple_of(qi * BQ, BQ)
    q = qkv_ref[0, pl.ds(q0, BQ), 0:G].astype(jnp.float32)
    c_q = c_ref[0]
    lane = lax.broadcasted_iota(jnp.int32, (1, G), 1)
    qrow = lax.broadcasted_iota(jnp.int32, (BQ, BQ), 0)
    kcol = lax.broadcasted_iota(jnp.int32, (BQ, BQ), 1)
    causal = kcol <= qrow
    out = jnp.zeros((BQ, G), jnp.float32)
    for h in range(N_GROUP_HEADS):
        hm = (lane // HEAD_DIM) == h
        qh = jnp.where(hm, q, 0.0).astype(jnp.bfloat16)
        cq = c_q[:, h:h + 1]

        def kv_step(j, carry, diagonal, h=h, qh=qh, cq=cq):
            m, l, acc = carry
            k0 = pl.multiple_of(j * BQ, BQ)
            k = qkv_ref[0, pl.ds(k0, BQ), G:2 * G]
            v = qkv_ref[0, pl.ds(k0, BQ), 2 * G:3 * G]
            s = _dot_nt(qh, k) - ct_ref[h:h + 1, pl.ds(k0, BQ)]
            if diagonal:
                s = jnp.where(causal, s, NEG_INF)
            m_new = jnp.maximum(m, jnp.max(s, axis=1, keepdims=True) + cq)
            p = jnp.exp(s + (cq - m_new))
            alpha = jnp.exp(m - m_new)
            l = alpha * l + jnp.sum(p, axis=1, keepdims=True)
            acc = alpha * acc + _dot(p.astype(jnp.bfloat16), v)
            return m_new, l, acc

        init = (jnp.full((BQ, 1), NEG_INF, jnp.float32), jnp.zeros((BQ, 1), jnp.float32),
                jnp.zeros((BQ, G), jnp.float32))
        carry = lax.fori_loop(0, qi, functools.partial(kv_step, diagonal=False), init)
        _, l, acc = kv_step(qi, carry, diagonal=True)
        out = jnp.where(hm, acc / l, out)
    out_ref[0] = out.astype(out_ref.dtype)


def _fox(qkvc, c, ct, batch, seq):
    G = GROUP_WIDTH
    BQ = FOX_BLOCK
    return pl.pallas_call(
        _fox_kernel,
        grid=(batch, seq // BQ),
        in_specs=[pl.BlockSpec((1, seq, 3 * G), lambda b, i: (b, 0, 0)),
                  pl.BlockSpec((1, BQ, LANES), lambda b, i: (b, i, 0)),
                  pl.BlockSpec((SUBLANES, seq), lambda b, i: (0, b))],
        out_specs=pl.BlockSpec((1, BQ, G), lambda b, i: (b, i, 0)),
        out_shape=jax.ShapeDtypeStruct((batch, seq, G), jnp.bfloat16),
        compiler_params=_params(("arbitrary", "arbitrary")),
        name="fox",
    )(qkvc.reshape(batch, seq, 3 * G), c.reshape(batch, seq, LANES), ct)


def _mem_kv_kernel(mem_ref, g_ref, w_ref, kv_ref):
    memn = _rms_norm(mem_ref[0], g_ref[...]).astype(jnp.bfloat16)
    kv = _dot(memn, w_ref[...])
    kv_ref[0, :, 0:D_MODEL] = (kv[:, 0:D_MODEL] * XA_HEAD_DIM ** -0.5).astype(kv_ref.dtype)
    kv_ref[0, :, D_MODEL:] = kv[:, D_MODEL:].astype(kv_ref.dtype)


def _mem_kv(mem, g, w_xkv):
    batch, mem_len, _ = mem.shape
    return pl.pallas_call(
        _mem_kv_kernel,
        grid=(batch,),
        in_specs=[pl.BlockSpec((1, mem_len, D_MODEL), lambda b: (b, 0, 0)),
                  _const_spec((1, D_MODEL)), _const_spec((D_MODEL, 2 * D_MODEL))],
        out_specs=pl.BlockSpec((1, mem_len, 2 * D_MODEL), lambda b: (b, 0, 0)),
        out_shape=jax.ShapeDtypeStruct((batch, mem_len, 2 * D_MODEL), jnp.bfloat16),
        compiler_params=_params(("arbitrary",)),
        name="mem_kv",
    )(mem, g, w_xkv)


def _mix_out_kernel(h_ref, ya_ref, yb_ref, yc_ref, yd_ref, wout_ref, g_ref, wq_ref, kv_ref, wo_ref,
                    out_ref):
    y = jnp.concatenate([ya_ref[...], yb_ref[...], yc_ref[...], yd_ref[...]], axis=1)
    h1 = h_ref[...] + _dot(y, wout_ref[...])
    q = _dot(_rms_norm(h1, g_ref[...]).astype(jnp.bfloat16), wq_ref[...])
    heads = []
    for hh in range(XA_HEADS):
        lo = hh * XA_HEAD_DIM
        qh = q[:, lo:lo + XA_HEAD_DIM].astype(jnp.bfloat16)
        s = _dot_nt(qh, kv_ref[0, :, lo:lo + XA_HEAD_DIM])
        p = jnp.exp(s - jnp.max(s, axis=1, keepdims=True))
        o = _dot(p.astype(jnp.bfloat16), kv_ref[0, :, D_MODEL + lo:D_MODEL + lo + XA_HEAD_DIM])
        heads.append((o / jnp.sum(p, axis=1, keepdims=True)).astype(jnp.bfloat16))
    out_ref[...] = h1 + _dot(jnp.concatenate(heads, axis=1), wo_ref[...])


def _mix_out(h, ya, yb, yc, yd, w_out, g_xa, w_xq, kv, w_xo, seq):
    n_tok = h.shape[0]
    tm = TOKEN_TILE
    G = GROUP_WIDTH
    mem_len = kv.shape[1]
    tiles_per_batch = seq // tm
    row_spec = lambda width: pl.BlockSpec((tm, width), lambda i: (i, 0))
    return pl.pallas_call(
        _mix_out_kernel,
        grid=(n_tok // tm,),
        in_specs=[row_spec(D_MODEL), row_spec(G), row_spec(G), row_spec(G), row_spec(G),
                  _const_spec((D_MODEL, D_MODEL)), _const_spec((1, D_MODEL)),
                  _const_spec((D_MODEL, D_MODEL)),
                  pl.BlockSpec((1, mem_len, 2 * D_MODEL), lambda i: (i // tiles_per_batch, 0, 0)),
                  _const_spec((D_MODEL, D_MODEL))],
        out_specs=row_spec(D_MODEL),
        out_shape=jax.ShapeDtypeStruct((n_tok, D_MODEL), jnp.float32),
        compiler_params=_params(("arbitrary",)),
        name="mix_out",
    )(h, ya, yb, yc, yd, w_out, g_xa, w_xq, kv, w_xo)


def _ffn_kernel(h_ref, g_ref, wup_ref, wconv_ref, wdown_ref, gfin_ref, out_ref, act_s, halo_s,
                *, tiles_per_batch, final_norm):
    tm = h_ref.shape[0]
    t = pl.program_id(0) % tiles_per_batch

    @pl.when(t == 0)
    def _():
        halo_s[...] = jnp.zeros_like(halo_s)

    h = h_ref[...]
    xn = _rms_norm(h, g_ref[...]).astype(jnp.bfloat16)

    def conv_branch(col):
        cols = slice(col, col + FFN_CHUNK)
        u = _dot(xn, wup_ref[:, cols])
        ext = jnp.concatenate([halo_s[:, cols], u], axis=0)
        y = (wconv_ref[2:3, cols] * ext + wconv_ref[1:2, cols] * _shift_rows(ext, 1)
             + wconv_ref[0:1, cols] * _shift_rows(ext, 2))
        halo_s[:, cols] = u[tm - SUBLANES:, :]
        return y[SUBLANES:, :]

    for col in range(0, D_FF, FFN_CHUNK):
        a = conv_branch(col)
        gate = conv_branch(D_FF + col)
        act_s[:, col:col + FFN_CHUNK] = (a * (gate * jax.nn.sigmoid(gate))).astype(act_s.dtype)

    out = h + _dot(act_s[...], wdown_ref[...])
    if final_norm:
        out = _rms_norm(out, gfin_ref[...])
    out_ref[...] = out


def _ffn(h, g_ffn, w_up, w_conv, w_down, g_final, seq, final_norm):
    n_tok = h.shape[0]
    tm = TOKEN_TILE
    row_spec = pl.BlockSpec((tm, D_MODEL), lambda i: (i, 0))
    return pl.pallas_call(
        functools.partial(_ffn_kernel, tiles_per_batch=seq // tm, final_norm=final_norm),
        grid=(n_tok // tm,),
        in_specs=[row_spec, _const_spec((1, D_MODEL)), _const_spec((D_MODEL, 2 * D_FF)),
                  _const_spec((SUBLANES, 2 * D_FF)), _const_spec((D_FF, D_MODEL)),
                  _const_spec((1, D_MODEL))],
        out_specs=row_spec,
        out_shape=jax.ShapeDtypeStruct((n_tok, D_MODEL), jnp.float32),
        scratch_shapes=[pltpu.VMEM((tm, D_FF), jnp.bfloat16),
                        pltpu.VMEM((SUBLANES, 2 * D_FF), jnp.float32)],
        compiler_params=_params(("arbitrary",)),
        name="ffn",
    )(h, g_ffn, w_up, w_conv, w_down, g_final)


def _pad_rows(w, rows):
    return jnp.concatenate([w, jnp.zeros((rows - w.shape[0],) + w.shape[1:], w.dtype)], axis=0)


def kernel(x, mem, positions, g_mix, w_in, b_forget, w_sconv, w_pool, pool_scale, w_out, g_xa, g_mem,
           w_xq, w_xkv, w_xo, g_ffn, w_up, w_ffconv, w_down, g_final):
    batch, seq, _ = x.shape
    depth = w_in.shape[0]
    G = GROUP_WIDTH
    bf16 = jnp.bfloat16
    n_tok = batch * seq

    cos_t, sin_t = _rope_tables(positions)
    pwin = jnp.repeat(jnp.asarray(POOL_WINDOWS, jnp.int32), POOL_GROUP).reshape(1, G)
    h = x.reshape(n_tok, D_MODEL)
    for l in range(depth):
        n_fox = 9 * G
        w_in_p = jnp.concatenate(
            [w_in[l][:, :n_fox], w_in[l][:, n_fox + N_GROUP_HEADS:], w_in[l][:, n_fox:n_fox + N_GROUP_HEADS],
             jnp.zeros((D_MODEL, LANES - N_GROUP_HEADS), w_in.dtype)], axis=1).astype(bf16)
        bf = jnp.concatenate([b_forget[l], jnp.zeros((LANES - N_GROUP_HEADS,), b_forget.dtype)]).reshape(1, LANES)
        wpool_bd = jax.scipy.linalg.block_diag(*[w_pool[l, g] for g in range(len(POOL_WINDOWS))]).astype(bf16)

        ya, qkvb, qkvc, yd, c, ct = _mix_in(
            h, g_mix[l].reshape(1, D_MODEL), w_in_p, cos_t, sin_t, _pad_rows(w_sconv[l], SUBLANES),
            wpool_bd, pool_scale[l].reshape(1, G), bf, pwin, seq)
        yb = _dswa(qkvb, batch, seq).reshape(n_tok, G)
        yc = _fox(qkvc, c, ct, batch, seq).reshape(n_tok, G)
        kv = _mem_kv(mem, g_mem[l].reshape(1, D_MODEL), w_xkv[l].astype(bf16))
        h = _mix_out(h, ya, yb, yc, yd, w_out[l].astype(bf16), g_xa[l].reshape(1, D_MODEL),
                     w_xq[l].astype(bf16), kv, w_xo[l].astype(bf16), seq)
        h = _ffn(h, g_ffn[l].reshape(1, D_MODEL), w_up[l].astype(bf16), _pad_rows(w_ffconv[l], SUBLANES),
                 w_down[l].astype(bf16), g_final.reshape(1, D_MODEL), seq, final_norm=(l == depth - 1))
    return h.reshape(batch, seq, D_MODEL)
```

```python
import functools

import jax
import jax.numpy as jnp
from jax import lax
from jax.experimental import pallas as pl
from jax.experimental.pallas import tpu as pltpu

D_MODEL = 1024
HEAD_DIM = 64
GROUP_WIDTH = D_MODEL // 4
N_GROUP_HEADS = GROUP_WIDTH // HEAD_DIM
DSWA_CONFIGS = ((128, 1), (512, 4), (2048, 16))
POOL_WINDOWS = (2, 4, 8, 16)
POOL_GROUP = GROUP_WIDTH // len(POOL_WINDOWS)
ROPE_THETA = 500000.0
ROPE_DIM = HEAD_DIM // 4
XA_HEADS = 4
XA_HEAD_DIM = D_MODEL // XA_HEADS
D_FF = ((8 * D_MODEL // 3 + 127) // 128) * 128
RMS_EPS = 1e-6
NEG_INF = -1e30
LOG2_E = 1.4426950408889634

LANES = 128
SUBLANES = 8
BF16_SUBLANES = 16
MXU_DIM = 256
VMEM_LIMIT_BYTES = 56 * 1024 * 1024

COL_CONV = 0
COL_DSWA = 3 * GROUP_WIDTH
COL_FOX = 6 * GROUP_WIDTH
COL_POOL = 9 * GROUP_WIDTH
COL_FORGET = 10 * GROUP_WIDTH
N_IN_PADDED = COL_FORGET + LANES

TOKEN_TILE = 512
DSWA_BLOCK = 128
FOX_BLOCK = 512
FFN_CHUNK = MXU_DIM

_NT = (((1,), (1,)), ((), ()))


def _dot(a, b):
    return jnp.dot(a, b, preferred_element_type=jnp.float32)


def _dot_nt(a, b):
    return lax.dot_general(a, b, _NT, preferred_element_type=jnp.float32)


def _rms_norm(x, g):
    return x * lax.rsqrt(jnp.mean(x * x, axis=-1, keepdims=True) + RMS_EPS) * g


def _params(semantics):
    return pltpu.CompilerParams(dimension_semantics=semantics, vmem_limit_bytes=VMEM_LIMIT_BYTES)


def _const_spec(shape):
    zeros = (0,) * len(shape)
    return pl.BlockSpec(shape, lambda *_: zeros)


def _rope_table_kernel(pos_ref, consts_ref, cos_ref, sin_ref):
    ang = pos_ref[...].astype(jnp.float32) * consts_ref[0:1, :]
    cos_ref[...] = jnp.where(consts_ref[1:2, :] > 0.0, jnp.cos(ang), 1.0)
    sin_ref[...] = jnp.sin(ang) * consts_ref[2:3, :]


def _rope_tables(positions):
    n_tok = positions.size
    lane = jnp.arange(LANES) % HEAD_DIM
    inv_freq = ROPE_THETA ** (-jnp.arange(0, ROPE_DIM, 2, dtype=jnp.float32) / ROPE_DIM)
    half = ROPE_DIM // 2
    consts = jnp.zeros((SUBLANES, LANES), jnp.float32)
    consts = consts.at[0].set(inv_freq[lane % half])
    consts = consts.at[1].set((lane < ROPE_DIM).astype(jnp.float32))
    consts = consts.at[2].set(jnp.where(lane < half, -1.0, jnp.where(lane < ROPE_DIM, 1.0, 0.0)))
    tile = 2048
    table = jax.ShapeDtypeStruct((n_tok, LANES), jnp.float32)
    return pl.pallas_call(
        _rope_table_kernel,
        grid=(n_tok // tile,),
        in_specs=[pl.BlockSpec((tile, 1), lambda i: (i, 0)), _const_spec((SUBLANES, LANES))],
        out_specs=[pl.BlockSpec((tile, LANES), lambda i: (i, 0))] * 2,
        out_shape=[table, table],
        compiler_params=_params(("arbitrary",)),
        name="rope_tables",
    )(positions.reshape(n_tok, 1), consts)


def _shift_rows(x, k):
    return pltpu.roll(x, k, 0)


def _mix_in_kernel(h_ref, g_ref, w_ref, cos_ref, sin_ref, wsc_ref, wpool_ref, pscale_ref, bf_ref,
                   pwin_ref, ya_ref, qkvb_ref, qkc_ref, vtc_ref, yd_ref, c_ref, ct_ref,
                   halo_a, halo_d, carry_c, *, tiles_per_batch):
    tm = h_ref.shape[0]
    G = GROUP_WIDTH
    t = pl.program_id(0) % tiles_per_batch

    @pl.when(t == 0)
    def _():
        halo_a[...] = jnp.zeros_like(halo_a)
        halo_d[...] = jnp.zeros_like(halo_d)
        carry_c[...] = jnp.zeros_like(carry_c)

    xn = _rms_norm(h_ref[...], g_ref[...]).astype(jnp.bfloat16)

    pa = _dot(xn, w_ref[:, COL_CONV:COL_CONV + 3 * G])
    x = pa[:, 2 * G:3 * G] * pa[:, 0:G]
    ext = jnp.concatenate([halo_a[...], x], axis=0)
    conv = (wsc_ref[2:3, :] * ext + wsc_ref[1:2, :] * _shift_rows(ext, 1)
            + wsc_ref[0:1, :] * _shift_rows(ext, 2))
    ya_ref[...] = (pa[:, G:2 * G] * conv[SUBLANES:, :]).astype(ya_ref.dtype)
    halo_a[...] = x[tm - SUBLANES:, :]

    pd = _dot(xn, w_ref[:, COL_POOL:COL_POOL + G])
    halo_rows = halo_d.shape[0]
    extd = jnp.concatenate([halo_d[...], pd], axis=0)
    s2 = extd + _shift_rows(extd, 1)
    s4 = s2 + _shift_rows(s2, 2)
    s8 = s4 + _shift_rows(s4, 4)
    s16 = s8 + _shift_rows(s8, 8)
    lane = lax.broadcasted_iota(jnp.int32, (1, G), 1)
    wsum = jnp.where(lane < POOL_GROUP, s2,
                     jnp.where(lane < 2 * POOL_GROUP, s4,
                               jnp.where(lane < 3 * POOL_GROUP, s8, s16)))[halo_rows:, :]
    tpos = t * tm + lax.broadcasted_iota(jnp.int32, (tm, 1), 0)
    cnt = jnp.minimum(tpos + 1, pwin_ref[...]).astype(jnp.float32)
    z = wsum / cnt - pd
    yd = _dot(z.astype(jnp.bfloat16), wpool_ref[...]) * pscale_ref[...]
    yd_ref[...] = yd.astype(yd_ref.dtype)
    halo_d[...] = pd[tm - halo_rows:, :]

    pf = _dot(xn, w_ref[:, COL_FORGET:COL_FORGET + LANES]) + bf_ref[...]
    c = (jnp.minimum(pf, 0.0) - jnp.log(1.0 + jnp.exp(-jnp.abs(pf)))) * LOG2_E
    row = lax.broadcasted_iota(jnp.int32, (tm, LANES), 0)
    k = 1
    while k < tm:
        c = c + jnp.where(row >= k, _shift_rows(c, k), 0.0)
        k *= 2
    c = c + carry_c[SUBLANES - 1:SUBLANES, :]
    c_ref[...] = c
    ct_ref[...] = c.T[0:SUBLANES, :]
    carry_c[...] = c[tm - SUBLANES:, :]

    cos2 = jnp.concatenate([cos_ref[...]] * (G // LANES), axis=1)
    sin2 = jnp.concatenate([sin_ref[...]] * (G // LANES), axis=1)
    first_half = (lane % HEAD_DIM) < (ROPE_DIM // 2)

    def rope(v):
        partner = jnp.where(first_half, pltpu.roll(v, G - ROPE_DIM // 2, 1),
                            pltpu.roll(v, ROPE_DIM // 2, 1))
        return v * cos2 + partner * sin2

    scale = HEAD_DIM ** -0.5
    pb = _dot(xn, w_ref[:, COL_DSWA:COL_DSWA + 3 * G])
    qkv_roped = (rope(pb[:, 0:G]) * scale, rope(pb[:, G:2 * G]), pb[:, 2 * G:3 * G])
    for part, val in enumerate(qkv_roped):
        for tile in range(G // LANES):
            qkvb_ref[part * (G // LANES) + tile] = val[:, tile * LANES:(tile + 1) * LANES]

    pc = _dot(xn, w_ref[:, COL_FOX:COL_FOX + 3 * G])
    qkc_ref[:, 0:G] = (pc[:, 0:G] * (scale * LOG2_E)).astype(qkc_ref.dtype)
    qkc_ref[:, G:2 * G] = pc[:, G:2 * G].astype(qkc_ref.dtype)
    vtc_ref[...] = pc[:, 2 * G:3 * G].T.astype(vtc_ref.dtype)


def _mix_in(h, g, w_in_p, cos_t, sin_t, wsc, wpool_bd, pscale, bf, pwin, seq):
    n_tok = h.shape[0]
    tm = TOKEN_TILE
    G = GROUP_WIDTH
    row_spec = lambda width: pl.BlockSpec((tm, width), lambda i: (i, 0))
    bf16 = jnp.bfloat16
    return pl.pallas_call(
        functools.partial(_mix_in_kernel, tiles_per_batch=seq // tm),
        grid=(n_tok // tm,),
        in_specs=[row_spec(D_MODEL), _const_spec((1, D_MODEL)), _const_spec((D_MODEL, N_IN_PADDED)),
                  row_spec(LANES), row_spec(LANES), _const_spec((SUBLANES, G)),
                  _const_spec((G, G)), _const_spec((1, G)), _const_spec((1, LANES)),
                  _const_spec((1, G))],
        out_specs=[row_spec(G), pl.BlockSpec((3 * G // LANES, tm, LANES), lambda i: (0, i, 0)),
                   row_spec(2 * G), pl.BlockSpec((G, tm), lambda i: (0, i)), row_spec(G), row_spec(LANES),
                   pl.BlockSpec((SUBLANES, tm), lambda i: (0, i))],
        out_shape=[jax.ShapeDtypeStruct((n_tok, G), bf16),
                   jax.ShapeDtypeStruct((3 * G // LANES, n_tok, LANES), jnp.float32),
                   jax.ShapeDtypeStruct((n_tok, 2 * G), bf16),
                   jax.ShapeDtypeStruct((G, n_tok), bf16),
                   jax.ShapeDtypeStruct((n_tok, G), bf16),
                   jax.ShapeDtypeStruct((n_tok, LANES), jnp.float32),
                   jax.ShapeDtypeStruct((SUBLANES, n_tok), jnp.float32)],
        scratch_shapes=[pltpu.VMEM((SUBLANES, G), jnp.float32),
                        pltpu.VMEM((2 * SUBLANES, G), jnp.float32),
                        pltpu.VMEM((SUBLANES, LANES), jnp.float32)],
        compiler_params=_params(("arbitrary",)),
        name="mix_in",
    )(h, g, w_in_p, cos_t, sin_t, wsc, wpool_bd, pscale, bf, pwin)


def _dswa_kernel(qkv_ref, out_ref, m_s, l_s, a_s):
    seq = qkv_ref.shape[1]
    QB = DSWA_BLOCK
    n_tiles = GROUP_WIDTH // LANES
    heads_per_tile = LANES // HEAD_DIM
    lane = lax.broadcasted_iota(jnp.int32, (1, LANES), 1)
    head_masks = [(lane // HEAD_DIM) == h for h in range(heads_per_tile)]
    qrow = lax.broadcasted_iota(jnp.int32, (QB, 2 * QB), 0)
    kcol = lax.broadcasted_iota(jnp.int32, (QB, 2 * QB), 1)
    band = (kcol >= qrow) & (kcol <= qrow + QB)

    for cfg, (window, dil) in enumerate(DSWA_CONFIGS):
        assert window == QB * dil
        nblk = seq // (QB * dil)

        def block(n, carry, cfg=cfg, dil=dil, nblk=nblk):
            rho = n // nblk
            i = n % nblk
            start = rho + (QB * dil) * i
            prev = rho + (QB * dil) * jnp.maximum(i - 1, 0)
            if dil > 1:
                rows, prows = pl.ds(start, QB, stride=dil), pl.ds(prev, QB, stride=dil)
            else:
                rows, prows = pl.ds(pl.multiple_of(start, QB), QB), pl.ds(pl.multiple_of(prev, QB), QB)
            valid = band & ((kcol >= QB) | (i > 0))
            for tile in range(n_tiles):
                q = qkv_ref[tile, rows, :]
                k = jnp.concatenate([qkv_ref[n_tiles + tile, prows, :],
                                     qkv_ref[n_tiles + tile, rows, :]], axis=0)
                v = jnp.concatenate([qkv_ref[2 * n_tiles + tile, prows, :],
                                     qkv_ref[2 * n_tiles + tile, rows, :]], axis=0)
                kb = k.astype(jnp.bfloat16)
                probs = []
                m_full = jnp.zeros((QB, LANES), jnp.float32)
                l_full = jnp.zeros((QB, LANES), jnp.float32)
                for hm in head_masks:
                    qh = jnp.where(hm, q, 0.0).astype(jnp.bfloat16)
                    s = jnp.where(valid, _dot_nt(qh, kb), NEG_INF)
                    m = jnp.max(s, axis=1, keepdims=True)
                    p = jnp.exp(s - m)
                    l = jnp.sum(p, axis=1, keepdims=True)
                    probs.append(p.astype(jnp.bfloat16))
                    m_full = jnp.where(hm, m, m_full)
                    l_full = jnp.where(hm, l, l_full)
                p_all = jnp.concatenate(probs, axis=1)
                v_bd = jnp.concatenate(
                    [jnp.where(hm, v, 0.0).astype(jnp.bfloat16) for hm in head_masks], axis=0)
                acc = _dot(p_all, v_bd)
                if cfg == 0:
                    m_s[tile, rows, :] = m_full
                    l_s[tile, rows, :] = l_full
                    a_s[tile, rows, :] = acc
                else:
                    m_old = m_s[tile, rows, :]
                    m_new = jnp.maximum(m_old, m_full)
                    w_old = jnp.exp(m_old - m_new)
                    w_cur = jnp.exp(m_full - m_new)
                    m_s[tile, rows, :] = m_new
                    l_s[tile, rows, :] = l_s[tile, rows, :] * w_old + l_full * w_cur
                    a_s[tile, rows, :] = a_s[tile, rows, :] * w_old + acc * w_cur
            return carry

        lax.fori_loop(0, seq // QB, block, 0)

    chunk = TOKEN_TILE
    for tile in range(n_tiles):
        for r in range(0, seq, chunk):
            out_ref[0, r:r + chunk, tile * LANES:(tile + 1) * LANES] = (
                a_s[tile, r:r + chunk, :] / l_s[tile, r:r + chunk, :]).astype(out_ref.dtype)


def _dswa(qkvb, batch, seq):
    G = GROUP_WIDTH
    n_slabs = 3 * G // LANES
    return pl.pallas_call(
        _dswa_kernel,
        grid=(batch,),
        in_specs=[pl.BlockSpec((n_slabs, seq, LANES), lambda b: (0, b, 0))],
        out_specs=pl.BlockSpec((1, seq, G), lambda b: (b, 0, 0)),
        out_shape=jax.ShapeDtypeStruct((batch, seq, G), jnp.bfloat16),
        scratch_shapes=[pltpu.VMEM((G // LANES, seq, LANES), jnp.float32)] * 3,
        compiler_params=_params(("arbitrary",)),
        name="dswa",
    )(qkvb)


def _fox_kernel(q_ref, k_ref, vt_ref, c_ref, ct_ref, out_ref):
    G = GROUP_WIDTH
    BQ = FOX_BLOCK
    H = N_GROUP_HEADS
    qi = pl.program_id(1)
    q = q_ref[0].astype(jnp.float32)
    lane = lax.broadcasted_iota(jnp.int32, (1, G), 1)
    q_heads = [jnp.where((lane // HEAD_DIM) == h, q, 0.0).astype(jnp.bfloat16) for h in range(H)]
    c_q = [ct_ref[h:h + 1, :] for h in range(H)]
    krow = lax.broadcasted_iota(jnp.int32, (BQ, BQ), 0)
    qcol = lax.broadcasted_iota(jnp.int32, (BQ, BQ), 1)
    causal = krow <= qcol

    ones_rows = jnp.ones((BF16_SUBLANES, BQ), jnp.bfloat16)

    def kv_step(j, carry, diagonal):
        k0 = pl.multiple_of(j * BQ, BQ)
        k = k_ref[0, pl.ds(k0, BQ), :]
        c_k = c_ref[0, pl.ds(k0, BQ), :]

        def scores(h):
            return _dot_nt(k, q_heads[h]) - c_k[:, h:h + 1]

        new = []
        s_next = scores(0)
        for h in range(H):
            m, acc = carry[h]
            s = s_next
            if h + 1 < H:
                s_next = scores(h + 1)
            if diagonal:
                s = jnp.where(causal, s, NEG_INF)
            m_new = jnp.maximum(m, jnp.max(s, axis=0, keepdims=True) + c_q[h])
            p = jnp.exp2(s + (c_q[h] - m_new)).astype(jnp.bfloat16)
            v_t = jnp.concatenate(
                [vt_ref[h * HEAD_DIM:(h + 1) * HEAD_DIM, pl.ds(k0, BQ)], ones_rows], axis=0)
            acc = jnp.exp2(m - m_new) * acc + _dot(v_t, p)
            new.append((m_new, acc))
        return tuple(new)

    init = tuple((jnp.full((1, BQ), NEG_INF, jnp.float32),
                  jnp.zeros((HEAD_DIM + BF16_SUBLANES, BQ), jnp.float32)) for _ in range(H))
    carry = lax.fori_loop(0, qi, functools.partial(kv_step, diagonal=False), init)
    final = kv_step(qi, carry, diagonal=True)
    out_t = jnp.concatenate([acc[0:HEAD_DIM] / acc[HEAD_DIM:HEAD_DIM + 1] for _, acc in final],
                            axis=0)
    out_ref[0] = out_t.T.astype(out_ref.dtype)


def _fox(qk, vt, c, ct, batch, seq):
    G = GROUP_WIDTH
    BQ = FOX_BLOCK
    nq = seq // BQ
    return pl.pallas_call(
        _fox_kernel,
        grid=(batch, nq),
        in_specs=[pl.BlockSpec((1, BQ, G), lambda b, i: (b, i, 0)),
                  pl.BlockSpec((1, seq, G), lambda b, i: (b, 0, 1)),
                  pl.BlockSpec((G, seq), lambda b, i: (0, b)),
                  pl.BlockSpec((1, seq, LANES), lambda b, i: (b, 0, 0)),
                  pl.BlockSpec((SUBLANES, BQ), lambda b, i: (0, b * nq + i))],
        out_specs=pl.BlockSpec((1, BQ, G), lambda b, i: (b, i, 0)),
        out_shape=jax.ShapeDtypeStruct((batch, seq, G), jnp.bfloat16),
        compiler_params=_params(("arbitrary", "arbitrary")),
        name="fox",
    )(qk.reshape(batch, seq, 2 * G), qk.reshape(batch, seq, 2 * G), vt, c.reshape(batch, seq, LANES), ct)


def _mem_kv_kernel(mem_ref, g_ref, w_ref, kv_ref):
    memn = _rms_norm(mem_ref[0], g_ref[...]).astype(jnp.bfloat16)
    kv = _dot(memn, w_ref[...])
    kv_ref[0, :, 0:D_MODEL] = (kv[:, 0:D_MODEL] * XA_HEAD_DIM ** -0.5).astype(kv_ref.dtype)
    kv_ref[0, :, D_MODEL:] = kv[:, D_MODEL:].astype(kv_ref.dtype)


def _mem_kv(mem, g, w_xkv):
    batch, mem_len, _ = mem.shape
    return pl.pallas_call(
        _mem_kv_kernel,
        grid=(batch,),
        in_specs=[pl.BlockSpec((1, mem_len, D_MODEL), lambda b: (b, 0, 0)),
                  _const_spec((1, D_MODEL)), _const_spec((D_MODEL, 2 * D_MODEL))],
        out_specs=pl.BlockSpec((1, mem_len, 2 * D_MODEL), lambda b: (b, 0, 0)),
        out_shape=jax.ShapeDtypeStruct((batch, mem_len, 2 * D_MODEL), jnp.bfloat16),
        compiler_params=_params(("arbitrary",)),
        name="mem_kv",
    )(mem, g, w_xkv)


def _mix_out_kernel(h_ref, ya_ref, yb_ref, yc_ref, yd_ref, wout_ref, g_ref, wq_ref, kv_ref, wo_ref,
                    out_ref):
    y = jnp.concatenate([ya_ref[...], yb_ref[...], yc_ref[...], yd_ref[...]], axis=1)
    h1 = h_ref[...] + _dot(y, wout_ref[...])
    q = _dot(_rms_norm(h1, g_ref[...]).astype(jnp.bfloat16), wq_ref[...])
    heads = []
    for hh in range(XA_HEADS):
        lo = hh * XA_HEAD_DIM
        qh = q[:, lo:lo + XA_HEAD_DIM].astype(jnp.bfloat16)
        s = _dot_nt(qh, kv_ref[0, :, lo:lo + XA_HEAD_DIM])
        p = jnp.exp(s - jnp.max(s, axis=1, keepdims=True))
        o = _dot(p.astype(jnp.bfloat16), kv_ref[0, :, D_MODEL + lo:D_MODEL + lo + XA_HEAD_DIM])
        heads.append((o / jnp.sum(p, axis=1, keepdims=True)).astype(jnp.bfloat16))
    out_ref[...] = h1 + _dot(jnp.concatenate(heads, axis=1), wo_ref[...])


def _mix_out(h, ya, yb, yc, yd, w_out, g_xa, w_xq, kv, w_xo, seq):
    n_tok = h.shape[0]
    tm = TOKEN_TILE
    G = GROUP_WIDTH
    mem_len = kv.shape[1]
    tiles_per_batch = seq // tm
    row_spec = lambda width: pl.BlockSpec((tm, width), lambda i: (i, 0))
    return pl.pallas_call(
        _mix_out_kernel,
        grid=(n_tok // tm,),
        in_specs=[row_spec(D_MODEL), row_spec(G), row_spec(G), row_spec(G), row_spec(G),
                  _const_spec((D_MODEL, D_MODEL)), _const_spec((1, D_MODEL)),
                  _const_spec((D_MODEL, D_MODEL)),
                  pl.BlockSpec((1, mem_len, 2 * D_MODEL), lambda i: (i // tiles_per_batch, 0, 0)),
                  _const_spec((D_MODEL, D_MODEL))],
        out_specs=row_spec(D_MODEL),
        out_shape=jax.ShapeDtypeStruct((n_tok, D_MODEL), jnp.float32),
        compiler_params=_params(("arbitrary",)),
        name="mix_out",
    )(h, ya, yb, yc, yd, w_out, g_xa, w_xq, kv, w_xo)


def _ffn_kernel(h_ref, g_ref, wup_ref, wconv_ref, wdown_ref, gfin_ref, out_ref, act_s, halo_s,
                *, tiles_per_batch, final_norm):
    tm = h_ref.shape[0]
    t = pl.program_id(0) % tiles_per_batch

    @pl.when(t == 0)
    def _():
        halo_s[...] = jnp.zeros_like(halo_s)

    h = h_ref[...]
    xn = _rms_norm(h, g_ref[...]).astype(jnp.bfloat16)

    def conv_branch(col):
        cols = slice(col, col + FFN_CHUNK)
        u = _dot(xn, wup_ref[:, cols])
        ext = jnp.concatenate([halo_s[:, cols], u], axis=0)
        y = (wconv_ref[2:3, cols] * ext + wconv_ref[1:2, cols] * _shift_rows(ext, 1)
             + wconv_ref[0:1, cols] * _shift_rows(ext, 2))
        halo_s[:, cols] = u[tm - SUBLANES:, :]
        return y[SUBLANES:, :]

    for col in range(0, D_FF, FFN_CHUNK):
        a = conv_branch(col)
        gate = conv_branch(D_FF + col)
        act_s[:, col:col + FFN_CHUNK] = (a * (gate * jax.nn.sigmoid(gate))).astype(act_s.dtype)

    out = h + _dot(act_s[...], wdown_ref[...])
    if final_norm:
        out = _rms_norm(out, gfin_ref[...])
    out_ref[...] = out


def _ffn(h, g_ffn, w_up, w_conv, w_down, g_final, seq, final_norm):
    n_tok = h.shape[0]
    tm = TOKEN_TILE
    row_spec = pl.BlockSpec((tm, D_MODEL), lambda i: (i, 0))
    return pl.pallas_call(
        functools.partial(_ffn_kernel, tiles_per_batch=seq // tm, final_norm=final_norm),
        grid=(n_tok // tm,),
        in_specs=[row_spec, _const_spec((1, D_MODEL)), _const_spec((D_MODEL, 2 * D_FF)),
                  _const_spec((SUBLANES, 2 * D_FF)), _const_spec((D_FF, D_MODEL)),
                  _const_spec((1, D_MODEL))],
        out_specs=row_spec,
        out_shape=jax.ShapeDtypeStruct((n_tok, D_MODEL), jnp.float32),
        scratch_shapes=[pltpu.VMEM((tm, D_FF), jnp.bfloat16),
                        pltpu.VMEM((SUBLANES, 2 * D_FF), jnp.float32)],
        compiler_params=_params(("arbitrary",)),
        name="ffn",
    )(h, g_ffn, w_up, w_conv, w_down, g_final)


def _pad_rows(w, rows):
    return jnp.concatenate([w, jnp.zeros((rows - w.shape[0],) + w.shape[1:], w.dtype)], axis=0)


def kernel(x, mem, positions, g_mix, w_in, b_forget, w_sconv, w_pool, pool_scale, w_out, g_xa, g_mem,
           w_xq, w_xkv, w_xo, g_ffn, w_up, w_ffconv, w_down, g_final):
    batch, seq, _ = x.shape
    depth = w_in.shape[0]
    G = GROUP_WIDTH
    bf16 = jnp.bfloat16
    n_tok = batch * seq

    cos_t, sin_t = _rope_tables(positions)
    pwin = jnp.repeat(jnp.asarray(POOL_WINDOWS, jnp.int32), POOL_GROUP).reshape(1, G)
    h = x.reshape(n_tok, D_MODEL)
    for l in range(depth):
        n_fox = 9 * G
        w_in_p = jnp.concatenate(
            [w_in[l][:, :n_fox], w_in[l][:, n_fox + N_GROUP_HEADS:], w_in[l][:, n_fox:n_fox + N_GROUP_HEADS],
             jnp.zeros((D_MODEL, LANES - N_GROUP_HEADS), w_in.dtype)], axis=1).astype(bf16)
        bf = jnp.concatenate([b_forget[l], jnp.zeros((LANES - N_GROUP_HEADS,), b_forget.dtype)]).reshape(1, LANES)
        wpool_bd = jax.scipy.linalg.block_diag(*[w_pool[l, g] for g in range(len(POOL_WINDOWS))]).astype(bf16)

        ya, qkvb, qkc, vtc, yd, c, ct = _mix_in(
            h, g_mix[l].reshape(1, D_MODEL), w_in_p, cos_t, sin_t, _pad_rows(w_sconv[l], SUBLANES),
            wpool_bd, pool_scale[l].reshape(1, G), bf, pwin, seq)
        yb = _dswa(qkvb, batch, seq).reshape(n_tok, G)
        yc = _fox(qkc, vtc, c, ct, batch, seq).reshape(n_tok, G)
        kv = _mem_kv(mem, g_mem[l].reshape(1, D_MODEL), w_xkv[l].astype(bf16))
        h = _mix_out(h, ya, yb, yc, yd, w_out[l].astype(bf16), g_xa[l].reshape(1, D_MODEL),
                     w_xq[l].astype(bf16), kv, w_xo[l].astype(bf16), seq)
        h = _ffn(h, g_ffn[l].reshape(1, D_MODEL), w_up[l].astype(bf16), _pad_rows(w_ffconv[l], SUBLANES),
                 w_down[l].astype(bf16), g_final.reshape(1, D_MODEL), seq, final_norm=(l == depth - 1))
    return h.reshape(batch, seq, D_MODEL)
```

```python
import functools

import jax
import jax.numpy as jnp
from jax import lax
from jax.experimental import pallas as pl
from jax.experimental.pallas import tpu as pltpu

D_MODEL = 1024
HEAD_DIM = 64
GROUP_WIDTH = D_MODEL // 4
N_GROUP_HEADS = GROUP_WIDTH // HEAD_DIM
DSWA_CONFIGS = ((128, 1), (512, 4), (2048, 16))
POOL_WINDOWS = (2, 4, 8, 16)
POOL_GROUP = GROUP_WIDTH // len(POOL_WINDOWS)
ROPE_THETA = 500000.0
ROPE_DIM = HEAD_DIM // 4
XA_HEADS = 4
XA_HEAD_DIM = D_MODEL // XA_HEADS
D_FF = ((8 * D_MODEL // 3 + 127) // 128) * 128
RMS_EPS = 1e-6
NEG_INF = -1e30
LOG2_E = 1.4426950408889634

LANES = 128
SUBLANES = 8
BF16_SUBLANES = 16
MXU_DIM = 256
VMEM_LIMIT_BYTES = 56 * 1024 * 1024

COL_CONV = 0
COL_DSWA = 3 * GROUP_WIDTH
COL_FOX = 6 * GROUP_WIDTH
N_IN_MAIN = 9 * GROUP_WIDTH
TAIL_POOL = 0
TAIL_FORGET = GROUP_WIDTH
N_IN_TAIL = TAIL_FORGET + LANES

TOKEN_TILE = 512
DSWA_BLOCK = 128
DSWA_UNROLL = (4, 2, 2)
FOX_BLOCK = 512
FOX_QUERY_CHUNK = 512
FOX_LOOKAHEAD = 4
FFN_CHUNK = MXU_DIM

_NT = (((1,), (1,)), ((), ()))


def _dot(a, b):
    return jnp.dot(a, b, preferred_element_type=jnp.float32)


def _dot_nt(a, b):
    return lax.dot_general(a, b, _NT, preferred_element_type=jnp.float32)


def _rms_norm(x, g):
    return x * lax.rsqrt(jnp.mean(x * x, axis=-1, keepdims=True) + RMS_EPS) * g


def _params(semantics):
    return pltpu.CompilerParams(dimension_semantics=semantics, vmem_limit_bytes=VMEM_LIMIT_BYTES)


def _const_spec(shape):
    zeros = (0,) * len(shape)
    return pl.BlockSpec(shape, lambda *_: zeros)


def _layer_spec(shape, layer):
    index = (layer,) + (0,) * len(shape)
    return pl.BlockSpec((None,) + tuple(shape), lambda *_: index)


def _rope_table_kernel(pos_ref, consts_ref, cos_ref, sin_ref):
    ang = pos_ref[...].astype(jnp.float32) * consts_ref[0:1, :]
    cos_ref[...] = jnp.where(consts_ref[1:2, :] > 0.0, jnp.cos(ang), 1.0)
    sin_ref[...] = jnp.sin(ang) * consts_ref[2:3, :]


def _rope_tables(positions):
    n_tok = positions.size
    lane = jnp.arange(LANES) % HEAD_DIM
    inv_freq = ROPE_THETA ** (-jnp.arange(0, ROPE_DIM, 2, dtype=jnp.float32) / ROPE_DIM)
    half = ROPE_DIM // 2
    consts = jnp.zeros((SUBLANES, LANES), jnp.float32)
    consts = consts.at[0].set(inv_freq[lane % half])
    consts = consts.at[1].set((lane < ROPE_DIM).astype(jnp.float32))
    consts = consts.at[2].set(jnp.where(lane < half, -1.0, jnp.where(lane < ROPE_DIM, 1.0, 0.0)))
    tile = 2048
    table = jax.ShapeDtypeStruct((n_tok, LANES), jnp.float32)
    return pl.pallas_call(
        _rope_table_kernel,
        grid=(n_tok // tile,),
        in_specs=[pl.BlockSpec((tile, 1), lambda i: (i, 0)), _const_spec((SUBLANES, LANES))],
        out_specs=[pl.BlockSpec((tile, LANES), lambda i: (i, 0))] * 2,
        out_shape=[table, table],
        compiler_params=_params(("arbitrary",)),
        name="rope_tables",
    )(positions.reshape(n_tok, 1), consts)


def _shift_rows(x, k):
    return pltpu.roll(x, k, 0)


def _mix_in_kernel(h_ref, g_ref, w_ref, wt_ref, cos_ref, sin_ref, wsc_ref, wpool_ref, pscale_ref,
                   bf_ref, pwin_ref, ya_ref, qkvb_ref, qkc_ref, vtc_ref, yd_ref, c_ref, ct_ref,
                   halo_a, halo_d, carry_c, *, tiles_per_batch):
    tm = h_ref.shape[0]
    G = GROUP_WIDTH
    t = pl.program_id(0) % tiles_per_batch

    @pl.when(t == 0)
    def _():
        halo_a[...] = jnp.zeros_like(halo_a)
        halo_d[...] = jnp.zeros_like(halo_d)
        carry_c[...] = jnp.zeros_like(carry_c)

    xn = _rms_norm(h_ref[...], g_ref[...]).astype(jnp.bfloat16)

    pa = _dot(xn, w_ref[:, COL_CONV:COL_CONV + 3 * G])
    x = pa[:, 2 * G:3 * G] * pa[:, 0:G]
    ext = jnp.concatenate([halo_a[...], x], axis=0)
    conv = (wsc_ref[2:3, :] * ext + wsc_ref[1:2, :] * _shift_rows(ext, 1)
            + wsc_ref[0:1, :] * _shift_rows(ext, 2))
    ya_ref[...] = (pa[:, G:2 * G] * conv[SUBLANES:, :]).astype(ya_ref.dtype)
    halo_a[...] = x[tm - SUBLANES:, :]

    pd = _dot(xn, wt_ref[:, TAIL_POOL:TAIL_POOL + G])
    halo_rows = halo_d.shape[0]
    extd = jnp.concatenate([halo_d[...], pd], axis=0)
    s2 = extd + _shift_rows(extd, 1)
    s4 = s2 + _shift_rows(s2, 2)
    s8 = s4 + _shift_rows(s4, 4)
    s16 = s8 + _shift_rows(s8, 8)
    lane = lax.broadcasted_iota(jnp.int32, (1, G), 1)
    wsum = jnp.where(lane < POOL_GROUP, s2,
                     jnp.where(lane < 2 * POOL_GROUP, s4,
                               jnp.where(lane < 3 * POOL_GROUP, s8, s16)))[halo_rows:, :]
    tpos = t * tm + lax.broadcasted_iota(jnp.int32, (tm, 1), 0)
    cnt = jnp.minimum(tpos + 1, pwin_ref[...]).astype(jnp.float32)
    z = wsum / cnt - pd
    yd = _dot(z.astype(jnp.bfloat16), wpool_ref[...]) * pscale_ref[...]
    yd_ref[...] = yd.astype(yd_ref.dtype)
    halo_d[...] = pd[tm - halo_rows:, :]

    pf = _dot(xn, wt_ref[:, TAIL_FORGET:TAIL_FORGET + LANES]) + bf_ref[...]
    c = (jnp.minimum(pf, 0.0) - jnp.log(1.0 + jnp.exp(-jnp.abs(pf)))) * LOG2_E
    row = lax.broadcasted_iota(jnp.int32, (tm, LANES), 0)
    k = 1
    while k < tm:
        c = c + jnp.where(row >= k, _shift_rows(c, k), 0.0)
        k *= 2
    c = c + carry_c[SUBLANES - 1:SUBLANES, :]
    c_ref[...] = c
    ct_ref[...] = c.T[0:SUBLANES, :]
    carry_c[...] = c[tm - SUBLANES:, :]

    cos2 = jnp.concatenate([cos_ref[...]] * (G // LANES), axis=1)
    sin2 = jnp.concatenate([sin_ref[...]] * (G // LANES), axis=1)
    first_half = (lane % HEAD_DIM) < (ROPE_DIM // 2)

    def rope(v):
        partner = jnp.where(first_half, pltpu.roll(v, G - ROPE_DIM // 2, 1),
                            pltpu.roll(v, ROPE_DIM // 2, 1))
        return v * cos2 + partner * sin2

    scale = HEAD_DIM ** -0.5 * LOG2_E
    pb = _dot(xn, w_ref[:, COL_DSWA:COL_DSWA + 3 * G])
    qkv_roped = (rope(pb[:, 0:G]) * scale, rope(pb[:, G:2 * G]), pb[:, 2 * G:3 * G])
    for part, val in enumerate(qkv_roped):
        for tile in range(G // LANES):
            qkvb_ref[part * (G // LANES) + tile] = val[:, tile * LANES:(tile + 1) * LANES]

    pc = _dot(xn, w_ref[:, COL_FOX:COL_FOX + 3 * G])
    qkc_ref[:, 0:G] = (pc[:, 0:G] * scale).astype(qkc_ref.dtype)
    qkc_ref[:, G:2 * G] = pc[:, G:2 * G].astype(qkc_ref.dtype)
    vtc_ref[...] = pc[:, 2 * G:3 * G].T.astype(vtc_ref.dtype)


def _mix_in(h, g, w_main, w_tail, cos_t, sin_t, wsc, wpool_bd, pscale, bf, pwin, seq, layer):
    n_tok = h.shape[0]
    tm = TOKEN_TILE
    G = GROUP_WIDTH
    row_spec = lambda width: pl.BlockSpec((tm, width), lambda i: (i, 0))
    slab_spec = lambda n: pl.BlockSpec((n, tm, LANES), lambda i: (0, i, 0))
    bf16 = jnp.bfloat16
    return pl.pallas_call(
        functools.partial(_mix_in_kernel, tiles_per_batch=seq // tm),
        grid=(n_tok // tm,),
        in_specs=[row_spec(D_MODEL), _layer_spec((1, D_MODEL), layer),
                  _layer_spec((D_MODEL, N_IN_MAIN), layer), _layer_spec((D_MODEL, N_IN_TAIL), layer),
                  row_spec(LANES), row_spec(LANES), _layer_spec((SUBLANES, G), layer),
                  _layer_spec((G, G), layer), _layer_spec((1, G), layer),
                  _layer_spec((1, LANES), layer), _const_spec((1, G))],
        out_specs=[row_spec(G), slab_spec(3 * G // LANES), row_spec(2 * G),
                   pl.BlockSpec((G, tm), lambda i: (0, i)), row_spec(G), row_spec(LANES),
                   pl.BlockSpec((SUBLANES, tm), lambda i: (0, i))],
        out_shape=[jax.ShapeDtypeStruct((n_tok, G), bf16),
                   jax.ShapeDtypeStruct((3 * G // LANES, n_tok, LANES), jnp.float32),
                   jax.ShapeDtypeStruct((n_tok, 2 * G), bf16),
                   jax.ShapeDtypeStruct((G, n_tok), bf16),
                   jax.ShapeDtypeStruct((n_tok, G), bf16),
                   jax.ShapeDtypeStruct((n_tok, LANES), jnp.float32),
                   jax.ShapeDtypeStruct((SUBLANES, n_tok), jnp.float32)],
        scratch_shapes=[pltpu.VMEM((SUBLANES, G), jnp.float32),
                        pltpu.VMEM((2 * SUBLANES, G), jnp.float32),
                        pltpu.VMEM((SUBLANES, LANES), jnp.float32)],
        compiler_params=_params(("arbitrary",)),
        name="mix_in",
    )(h, g, w_main, w_tail, cos_t, sin_t, wsc, wpool_bd, pscale, bf, pwin)


def _dswa_kernel(qkv_ref, out_ref, m_s, l_s, a_s):
    seq = qkv_ref.shape[1]
    QB = DSWA_BLOCK
    n_tiles = GROUP_WIDTH // LANES
    heads_per_tile = LANES // HEAD_DIM
    lane = lax.broadcasted_iota(jnp.int32, (1, LANES), 1)
    head_masks = [(lane // HEAD_DIM) == h for h in range(heads_per_tile)]
    qrow = lax.broadcasted_iota(jnp.int32, (QB, 2 * QB), 0)
    kcol = lax.broadcasted_iota(jnp.int32, (QB, 2 * QB), 1)
    band = (kcol >= qrow) & (kcol <= qrow + QB)
    ones_bd = jnp.concatenate(
        [jnp.broadcast_to(jnp.where(hm, 1.0, 0.0), (2 * QB, LANES)).astype(jnp.bfloat16)
         for hm in head_masks], axis=0)

    for cfg, (window, dil) in enumerate(DSWA_CONFIGS):
        assert window == QB * dil
        nblk = seq // (QB * dil)
        unroll = DSWA_UNROLL[cfg]
        assert (seq // QB) % unroll == 0

        def blocks(it, carry, cfg=cfg, dil=dil, nblk=nblk, unroll=unroll):
            scored = []
            for u in range(unroll):
                n = it * unroll + u
                rho = n // nblk
                i = n % nblk
                start = rho + (QB * dil) * i
                prev = rho + (QB * dil) * jnp.maximum(i - 1, 0)
                if dil > 1:
                    rows, prows = pl.ds(start, QB, stride=dil), pl.ds(prev, QB, stride=dil)
                else:
                    rows = pl.ds(pl.multiple_of(start, QB), QB)
                    prows = pl.ds(pl.multiple_of(prev, QB), QB)
                valid = band & ((kcol >= QB) | (i > 0))
                for tile in range(n_tiles):
                    q = qkv_ref[tile, rows, :]
                    k = jnp.concatenate([qkv_ref[n_tiles + tile, prows, :],
                                         qkv_ref[n_tiles + tile, rows, :]], axis=0)
                    v = jnp.concatenate([qkv_ref[2 * n_tiles + tile, prows, :],
                                         qkv_ref[2 * n_tiles + tile, rows, :]], axis=0)
                    kb = k.astype(jnp.bfloat16)
                    s_heads = [_dot_nt(jnp.where(hm, q, 0.0).astype(jnp.bfloat16), kb)
                               for hm in head_masks]
                    scored.append((rows, valid, tile, s_heads, v))

            weighted = []
            for rows, valid, tile, s_heads, v in scored:
                probs, maxes = [], []
                for s in s_heads:
                    s = jnp.where(valid, s, NEG_INF)
                    m = jnp.max(s, axis=1, keepdims=True)
                    probs.append(jnp.exp2(s - m).astype(jnp.bfloat16))
                    maxes.append(m)
                m_full = jnp.where(head_masks[0], maxes[0], maxes[1])
                v_bd = jnp.concatenate(
                    [jnp.where(hm, v, 0.0).astype(jnp.bfloat16) for hm in head_masks], axis=0)
                weighted.append((rows, tile, jnp.concatenate(probs, axis=1), m_full,
                                 jnp.concatenate([v_bd, ones_bd], axis=1)))

            reduced = [(rows, tile, m_full, _dot(p_all, v_aug))
                       for rows, tile, p_all, m_full, v_aug in weighted]

            for rows, tile, m_full, acc_l in reduced:
                acc, l_full = acc_l[:, 0:LANES], acc_l[:, LANES:2 * LANES]
                if cfg == 0:
                    m_s[tile, rows, :] = m_full
                    l_s[tile, rows, :] = l_full
                    a_s[tile, rows, :] = acc
                else:
                    m_old = m_s[tile, rows, :]
                    m_new = jnp.maximum(m_old, m_full)
                    w_old = jnp.exp2(m_old - m_new)
                    w_cur = jnp.exp2(m_full - m_new)
                    m_s[tile, rows, :] = m_new
                    l_s[tile, rows, :] = l_s[tile, rows, :] * w_old + l_full * w_cur
                    a_s[tile, rows, :] = a_s[tile, rows, :] * w_old + acc * w_cur
            return carry

        lax.fori_loop(0, seq // (QB * unroll), blocks, 0)

    chunk = TOKEN_TILE
    for tile in range(n_tiles):
        for r in range(0, seq, chunk):
            out_ref[0, r:r + chunk, tile * LANES:(tile + 1) * LANES] = (
                a_s[tile, r:r + chunk, :] / l_s[tile, r:r + chunk, :]).astype(out_ref.dtype)


def _dswa(qkvb, batch, seq):
    G = GROUP_WIDTH
    n_slabs = 3 * G // LANES
    return pl.pallas_call(
        _dswa_kernel,
        grid=(batch,),
        in_specs=[pl.BlockSpec((n_slabs, seq, LANES), lambda b: (0, b, 0))],
        out_specs=pl.BlockSpec((1, seq, G), lambda b: (b, 0, 0)),
        out_shape=jax.ShapeDtypeStruct((batch, seq, G), jnp.bfloat16),
        scratch_shapes=[pltpu.VMEM((G // LANES, seq, LANES), jnp.float32)] * 3,
        compiler_params=_params(("arbitrary",)),
        name="dswa",
    )(qkvb)


def _fox_kernel(q_ref, k_ref, vt_ref, c_ref, ct_ref, out_ref):
    G = GROUP_WIDTH
    BQ = FOX_BLOCK
    H = N_GROUP_HEADS
    qi = pl.program_id(1)
    q = q_ref[0].astype(jnp.float32)
    lane = lax.broadcasted_iota(jnp.int32, (1, G), 1)
    q_heads = [jnp.where((lane // HEAD_DIM) == h, q, 0.0).astype(jnp.bfloat16) for h in range(H)]
    c_q = [ct_ref[h:h + 1, :] for h in range(H)]
    krow = lax.broadcasted_iota(jnp.int32, (BQ, BQ), 0)
    qcol = lax.broadcasted_iota(jnp.int32, (BQ, BQ), 1)
    causal = krow <= qcol

    ones_rows = jnp.ones((BF16_SUBLANES, BQ), jnp.bfloat16)
    QC = FOX_QUERY_CHUNK
    chains = [(h, c * QC) for h in range(H) for c in range(BQ // QC)]

    def kv_step(j, carry, diagonal):
        k0 = pl.multiple_of(j * BQ, BQ)
        k = k_ref[0, pl.ds(k0, BQ), :]
        c_k = c_ref[0, pl.ds(k0, BQ), :]

        def scores(h, q0):
            return _dot_nt(k, q_heads[h][q0:q0 + QC]) - c_k[:, h:h + 1]

        pending = [scores(*chains[n]) for n in range(FOX_LOOKAHEAD)]
        new = []
        for n, (h, q0) in enumerate(chains):
            m, acc = carry[n]
            s = pending.pop(0)
            if n + FOX_LOOKAHEAD < len(chains):
                pending.append(scores(*chains[n + FOX_LOOKAHEAD]))
            if diagonal:
                s = jnp.where(causal[:, q0:q0 + QC], s, NEG_INF)
            cq = c_q[h][:, q0:q0 + QC]
            m_new = jnp.maximum(m, jnp.max(s, axis=0, keepdims=True) + cq)
            p = jnp.exp2(s + (cq - m_new)).astype(jnp.bfloat16)
            v_t = jnp.concatenate(
                [vt_ref[h * HEAD_DIM:(h + 1) * HEAD_DIM, pl.ds(k0, BQ)], ones_rows], axis=0)
            acc = jnp.exp2(m - m_new) * acc + _dot(v_t, p)
            new.append((m_new, acc))
        return tuple(new)

    init = tuple((jnp.full((1, QC), NEG_INF, jnp.float32),
                  jnp.zeros((HEAD_DIM + BF16_SUBLANES, QC), jnp.float32)) for _ in chains)
    carry = lax.fori_loop(0, qi, functools.partial(kv_step, diagonal=False), init)
    final = kv_step(qi, carry, diagonal=True)
    per_chain = [acc[0:HEAD_DIM] / acc[HEAD_DIM:HEAD_DIM + 1] for _, acc in final]
    n_chunks = BQ // QC
    out_t = jnp.concatenate(
        [jnp.concatenate(per_chain[h * n_chunks:(h + 1) * n_chunks], axis=1) for h in range(H)],
        axis=0)
    out_ref[0] = out_t.T.astype(out_ref.dtype)


def _fox(qk, vt, c, ct, batch, seq):
    G = GROUP_WIDTH
    BQ = FOX_BLOCK
    nq = seq // BQ
    return pl.pallas_call(
        _fox_kernel,
        grid=(batch, nq),
        in_specs=[pl.BlockSpec((1, BQ, G), lambda b, i: (b, i, 0)),
                  pl.BlockSpec((1, seq, G), lambda b, i: (b, 0, 1)),
                  pl.BlockSpec((G, seq), lambda b, i: (0, b)),
                  pl.BlockSpec((1, seq, LANES), lambda b, i: (b, 0, 0)),
                  pl.BlockSpec((SUBLANES, BQ), lambda b, i: (0, b * nq + i))],
        out_specs=pl.BlockSpec((1, BQ, G), lambda b, i: (b, i, 0)),
        out_shape=jax.ShapeDtypeStruct((batch, seq, G), jnp.bfloat16),
        compiler_params=_params(("arbitrary", "arbitrary")),
        name="fox",
    )(qk.reshape(batch, seq, 2 * G), qk.reshape(batch, seq, 2 * G), vt, c.reshape(batch, seq, LANES), ct)


def _mem_kv_kernel(mem_ref, g_ref, w_ref, kv_ref):
    memn = _rms_norm(mem_ref[0], g_ref[...]).astype(jnp.bfloat16)
    kv = _dot(memn, w_ref[...])
    kv_ref[0, :, 0:D_MODEL] = (kv[:, 0:D_MODEL] * XA_HEAD_DIM ** -0.5).astype(kv_ref.dtype)
    kv_ref[0, :, D_MODEL:] = kv[:, D_MODEL:].astype(kv_ref.dtype)


def _mem_kv(mem, g, w_xkv, layer):
    batch, mem_len, _ = mem.shape
    return pl.pallas_call(
        _mem_kv_kernel,
        grid=(batch,),
        in_specs=[pl.BlockSpec((1, mem_len, D_MODEL), lambda b: (b, 0, 0)),
                  _layer_spec((1, D_MODEL), layer), _layer_spec((D_MODEL, 2 * D_MODEL), layer)],
        out_specs=pl.BlockSpec((1, mem_len, 2 * D_MODEL), lambda b: (b, 0, 0)),
        out_shape=jax.ShapeDtypeStruct((batch, mem_len, 2 * D_MODEL), jnp.bfloat16),
        compiler_params=_params(("arbitrary",)),
        name="mem_kv",
    )(mem, g, w_xkv)


def _mix_out_kernel(h_ref, ya_ref, yb_ref, yc_ref, yd_ref, wout_ref, g_ref, wq_ref, kv_ref, wo_ref,
                    out_ref):
    y = jnp.concatenate([ya_ref[...], yb_ref[...], yc_ref[...], yd_ref[...]], axis=1)
    h1 = h_ref[...] + _dot(y, wout_ref[...])
    q = _dot(_rms_norm(h1, g_ref[...]).astype(jnp.bfloat16), wq_ref[...])
    heads = []
    for hh in range(XA_HEADS):
        lo = hh * XA_HEAD_DIM
        qh = q[:, lo:lo + XA_HEAD_DIM].astype(jnp.bfloat16)
        s = _dot_nt(qh, kv_ref[0, :, lo:lo + XA_HEAD_DIM])
        p = jnp.exp(s - jnp.max(s, axis=1, keepdims=True))
        o = _dot(p.astype(jnp.bfloat16), kv_ref[0, :, D_MODEL + lo:D_MODEL + lo + XA_HEAD_DIM])
        heads.append((o / jnp.sum(p, axis=1, keepdims=True)).astype(jnp.bfloat16))
    out_ref[...] = h1 + _dot(jnp.concatenate(heads, axis=1), wo_ref[...])


def _mix_out(h, ya, yb, yc, yd, w_out, g_xa, w_xq, kv, w_xo, seq, layer):
    n_tok = h.shape[0]
    tm = TOKEN_TILE
    G = GROUP_WIDTH
    mem_len = kv.shape[1]
    tiles_per_batch = seq // tm
    row_spec = lambda width: pl.BlockSpec((tm, width), lambda i: (i, 0))
    return pl.pallas_call(
        _mix_out_kernel,
        grid=(n_tok // tm,),
        in_specs=[row_spec(D_MODEL), row_spec(G), row_spec(G), row_spec(G), row_spec(G),
                  _layer_spec((D_MODEL, D_MODEL), layer), _layer_spec((1, D_MODEL), layer),
                  _layer_spec((D_MODEL, D_MODEL), layer),
                  pl.BlockSpec((1, mem_len, 2 * D_MODEL), lambda i: (i // tiles_per_batch, 0, 0)),
                  _layer_spec((D_MODEL, D_MODEL), layer)],
        out_specs=row_spec(D_MODEL),
        out_shape=jax.ShapeDtypeStruct((n_tok, D_MODEL), jnp.float32),
        compiler_params=_params(("arbitrary",)),
        name="mix_out",
    )(h, ya, yb, yc, yd, w_out, g_xa, w_xq, kv, w_xo)


def _ffn_kernel(h_ref, g_ref, wup_ref, wconv_ref, wdown_ref, gfin_ref, out_ref, act_s, halo_s,
                *, tiles_per_batch, final_norm):
    tm = h_ref.shape[0]
    t = pl.program_id(0) % tiles_per_batch

    @pl.when(t == 0)
    def _():
        halo_s[...] = jnp.zeros_like(halo_s)

    h = h_ref[...]
    xn = _rms_norm(h, g_ref[...]).astype(jnp.bfloat16)

    def conv_branch(col):
        cols = slice(col, col + FFN_CHUNK)
        u = _dot(xn, wup_ref[:, cols])
        ext = jnp.concatenate([halo_s[:, cols], u], axis=0)
        y = (wconv_ref[2:3, cols] * ext + wconv_ref[1:2, cols] * _shift_rows(ext, 1)
             + wconv_ref[0:1, cols] * _shift_rows(ext, 2))
        halo_s[:, cols] = u[tm - SUBLANES:, :]
        return y[SUBLANES:, :]

    for col in range(0, D_FF, FFN_CHUNK):
        a = conv_branch(col)
        gate = conv_branch(D_FF + col)
        act_s[:, col:col + FFN_CHUNK] = (a * (gate * jax.nn.sigmoid(gate))).astype(act_s.dtype)

    out = h + _dot(act_s[...], wdown_ref[...])
    if final_norm:
        out = _rms_norm(out, gfin_ref[...])
    out_ref[...] = out


def _ffn(h, g_ffn, w_up, w_conv, w_down, g_final, seq, layer, final_norm):
    n_tok = h.shape[0]
    tm = TOKEN_TILE
    row_spec = pl.BlockSpec((tm, D_MODEL), lambda i: (i, 0))
    return pl.pallas_call(
        functools.partial(_ffn_kernel, tiles_per_batch=seq // tm, final_norm=final_norm),
        grid=(n_tok // tm,),
        in_specs=[row_spec, _layer_spec((1, D_MODEL), layer), _layer_spec((D_MODEL, 2 * D_FF), layer),
                  _layer_spec((SUBLANES, 2 * D_FF), layer), _layer_spec((D_FF, D_MODEL), layer),
                  _const_spec((1, D_MODEL))],
        out_specs=row_spec,
        out_shape=jax.ShapeDtypeStruct((n_tok, D_MODEL), jnp.float32),
        scratch_shapes=[pltpu.VMEM((tm, D_FF), jnp.bfloat16),
                        pltpu.VMEM((SUBLANES, 2 * D_FF), jnp.float32)],
        compiler_params=_params(("arbitrary",)),
        name="ffn",
    )(h, g_ffn, w_up, w_conv, w_down, g_final)


def _pad_axis(w, size, axis):
    pad = [(0, 0)] * w.ndim
    pad[axis] = (0, size - w.shape[axis])
    return jnp.pad(w, pad)


def kernel(x, mem, positions, g_mix, w_in, b_forget, w_sconv, w_pool, pool_scale, w_out, g_xa, g_mem,
           w_xq, w_xkv, w_xo, g_ffn, w_up, w_ffconv, w_down, g_final):
    batch, seq, _ = x.shape
    depth = w_in.shape[0]
    G = GROUP_WIDTH
    bf16 = jnp.bfloat16
    n_tok = batch * seq

    row = lambda p: p.reshape(depth, 1, p.shape[-1])
    w_main = w_in[:, :, :N_IN_MAIN].astype(bf16)
    w_tail = _pad_axis(jnp.concatenate([w_in[:, :, N_IN_MAIN + N_GROUP_HEADS:],
                                        w_in[:, :, N_IN_MAIN:N_IN_MAIN + N_GROUP_HEADS]], axis=2),
                       N_IN_TAIL, 2).astype(bf16)
    bf = row(_pad_axis(b_forget, LANES, 1))
    n_groups = len(POOL_WINDOWS)
    wpool_bd = (w_pool[:, :, :, None, :] * jnp.eye(n_groups, dtype=w_pool.dtype)[None, :, None, :, None]
                ).reshape(depth, G, G).astype(bf16)
    wsc = _pad_axis(w_sconv, SUBLANES, 1)
    wfc = _pad_axis(w_ffconv, SUBLANES, 1)
    w_out_b, w_xq_b, w_xkv_b, w_xo_b = (w.astype(bf16) for w in (w_out, w_xq, w_xkv, w_xo))
    w_up_b, w_down_b = w_up.astype(bf16), w_down.astype(bf16)
    pwin = jnp.repeat(jnp.asarray(POOL_WINDOWS, jnp.int32), POOL_GROUP).reshape(1, G)

    cos_t, sin_t = _rope_tables(positions)
    h = x.reshape(n_tok, D_MODEL)
    for l in range(depth):
        ya, qkvb, qkc, vtc, yd, c, ct = _mix_in(
            h, row(g_mix), w_main, w_tail, cos_t, sin_t, wsc, wpool_bd, row(pool_scale), bf, pwin, seq, l)
        yb = _dswa(qkvb, batch, seq).reshape(n_tok, G)
        yc = _fox(qkc, vtc, c, ct, batch, seq).reshape(n_tok, G)
        kv = _mem_kv(mem, row(g_mem), w_xkv_b, l)
        h = _mix_out(h, ya, yb, yc, yd, w_out_b, row(g_xa), w_xq_b, kv, w_xo_b, seq, l)
        h = _ffn(h, row(g_ffn), w_up_b, wfc, w_down_b, g_final.reshape(1, D_MODEL), seq, l,
                 final_norm=(l == depth - 1))
    return h.reshape(batch, seq, D_MODEL)
```

```python
import functools

import jax
import jax.numpy as jnp
from jax import lax
from jax.experimental import pallas as pl
from jax.experimental.pallas import tpu as pltpu

D_MODEL = 1024
HEAD_DIM = 64
GROUP_WIDTH = D_MODEL // 4
N_GROUP_HEADS = GROUP_WIDTH // HEAD_DIM
DSWA_CONFIGS = ((128, 1), (512, 4), (2048, 16))
POOL_WINDOWS = (2, 4, 8, 16)
POOL_GROUP = GROUP_WIDTH // len(POOL_WINDOWS)
ROPE_THETA = 500000.0
ROPE_DIM = HEAD_DIM // 4
XA_HEADS = 4
XA_HEAD_DIM = D_MODEL // XA_HEADS
D_FF = ((8 * D_MODEL // 3 + 127) // 128) * 128
RMS_EPS = 1e-6
NEG_INF = -1e30
LOG2_E = 1.4426950408889634

LANES = 128
SUBLANES = 8
BF16_SUBLANES = 16
MXU_DIM = 256
VMEM_LIMIT_BYTES = 56 * 1024 * 1024

COL_CONV = 0
COL_DSWA = 3 * GROUP_WIDTH
COL_FOX = 6 * GROUP_WIDTH
N_IN_MAIN = 9 * GROUP_WIDTH
TAIL_POOL = 0
TAIL_FORGET = GROUP_WIDTH
N_IN_TAIL = TAIL_FORGET + LANES

TOKEN_TILE = 512
DSWA_BLOCK = 128
DSWA_UNROLL = (4, 2, 2)
FOX_BLOCK = 512
FOX_LOOKAHEAD = 3
FFN_CHUNK = MXU_DIM

_NT = (((1,), (1,)), ((), ()))


def _dot(a, b):
    return jnp.dot(a, b, preferred_element_type=jnp.float32)


def _dot_nt(a, b):
    return lax.dot_general(a, b, _NT, preferred_element_type=jnp.float32)


def _rms_norm(x, g):
    return x * lax.rsqrt(jnp.mean(x * x, axis=-1, keepdims=True) + RMS_EPS) * g


def _params(semantics):
    return pltpu.CompilerParams(dimension_semantics=semantics, vmem_limit_bytes=VMEM_LIMIT_BYTES)


def _const_spec(shape):
    zeros = (0,) * len(shape)
    return pl.BlockSpec(shape, lambda *_: zeros)


def _layer_spec(shape, layer):
    index = (layer,) + (0,) * len(shape)
    return pl.BlockSpec((None,) + tuple(shape), lambda *_: index)


def _rope_table_kernel(pos_ref, consts_ref, cos_ref, sin_ref):
    ang = pos_ref[...].astype(jnp.float32) * consts_ref[0:1, :]
    cos_ref[...] = jnp.where(consts_ref[1:2, :] > 0.0, jnp.cos(ang), 1.0)
    sin_ref[...] = jnp.sin(ang) * consts_ref[2:3, :]


def _rope_tables(positions):
    n_tok = positions.size
    lane = jnp.arange(LANES) % HEAD_DIM
    inv_freq = ROPE_THETA ** (-jnp.arange(0, ROPE_DIM, 2, dtype=jnp.float32) / ROPE_DIM)
    half = ROPE_DIM // 2
    consts = jnp.zeros((SUBLANES, LANES), jnp.float32)
    consts = consts.at[0].set(inv_freq[lane % half])
    consts = consts.at[1].set((lane < ROPE_DIM).astype(jnp.float32))
    consts = consts.at[2].set(jnp.where(lane < half, -1.0, jnp.where(lane < ROPE_DIM, 1.0, 0.0)))
    tile = 2048
    table = jax.ShapeDtypeStruct((n_tok, LANES), jnp.float32)
    return pl.pallas_call(
        _rope_table_kernel,
        grid=(n_tok // tile,),
        in_specs=[pl.BlockSpec((tile, 1), lambda i: (i, 0)), _const_spec((SUBLANES, LANES))],
        out_specs=[pl.BlockSpec((tile, LANES), lambda i: (i, 0))] * 2,
        out_shape=[table, table],
        compiler_params=_params(("arbitrary",)),
        name="rope_tables",
    )(positions.reshape(n_tok, 1), consts)


def _shift_rows(x, k):
    return pltpu.roll(x, k, 0)


def _mix_in_kernel(h_ref, g_ref, w_ref, wt_ref, cos_ref, sin_ref, wsc_ref, wpool_ref, pscale_ref,
                   bf_ref, pwin_ref, ya_ref, qkvb_ref, qkc_ref, vtc_ref, yd_ref, c_ref, ct_ref,
                   halo_a, halo_d, carry_c, *, tiles_per_batch):
    tm = h_ref.shape[0]
    G = GROUP_WIDTH
    t = pl.program_id(0) % tiles_per_batch

    @pl.when(t == 0)
    def _():
        halo_a[...] = jnp.zeros_like(halo_a)
        halo_d[...] = jnp.zeros_like(halo_d)
        carry_c[...] = jnp.zeros_like(carry_c)

    xn = _rms_norm(h_ref[...], g_ref[...]).astype(jnp.bfloat16)

    pa = _dot(xn, w_ref[:, COL_CONV:COL_CONV + 3 * G])
    x = pa[:, 2 * G:3 * G] * pa[:, 0:G]
    ext = jnp.concatenate([halo_a[...], x], axis=0)
    conv = (wsc_ref[2:3, :] * ext + wsc_ref[1:2, :] * _shift_rows(ext, 1)
            + wsc_ref[0:1, :] * _shift_rows(ext, 2))
    ya_ref[...] = (pa[:, G:2 * G] * conv[SUBLANES:, :]).astype(ya_ref.dtype)
    halo_a[...] = x[tm - SUBLANES:, :]

    pd = _dot(xn, wt_ref[:, TAIL_POOL:TAIL_POOL + G])
    halo_rows = halo_d.shape[0]
    extd = jnp.concatenate([halo_d[...], pd], axis=0)
    s2 = extd + _shift_rows(extd, 1)
    s4 = s2 + _shift_rows(s2, 2)
    s8 = s4 + _shift_rows(s4, 4)
    s16 = s8 + _shift_rows(s8, 8)
    lane = lax.broadcasted_iota(jnp.int32, (1, G), 1)
    wsum = jnp.where(lane < POOL_GROUP, s2,
                     jnp.where(lane < 2 * POOL_GROUP, s4,
                               jnp.where(lane < 3 * POOL_GROUP, s8, s16)))[halo_rows:, :]
    tpos = t * tm + lax.broadcasted_iota(jnp.int32, (tm, 1), 0)
    cnt = jnp.minimum(tpos + 1, pwin_ref[...]).astype(jnp.float32)
    z = (wsum / cnt - pd).astype(jnp.bfloat16)
    halo_d[...] = pd[tm - halo_rows:, :]

    pf = _dot(xn, wt_ref[:, TAIL_FORGET:TAIL_FORGET + LANES]) + bf_ref[...]
    c = (jnp.minimum(pf, 0.0) - jnp.log(1.0 + jnp.exp(-jnp.abs(pf)))) * LOG2_E
    row = lax.broadcasted_iota(jnp.int32, (tm, LANES), 0)
    k = 1
    while k < tm:
        c = c + jnp.where(row >= k, _shift_rows(c, k), 0.0)
        k *= 2
    c = c + carry_c[SUBLANES - 1:SUBLANES, :]
    c_ref[...] = c
    ct_ref[...] = c.T[0:SUBLANES, :]
    carry_c[...] = c[tm - SUBLANES:, :]

    cos2 = jnp.concatenate([cos_ref[...]] * (G // LANES), axis=1)
    sin2 = jnp.concatenate([sin_ref[...]] * (G // LANES), axis=1)
    first_half = (lane % HEAD_DIM) < (ROPE_DIM // 2)

    def rope(v):
        partner = jnp.where(first_half, pltpu.roll(v, G - ROPE_DIM // 2, 1),
                            pltpu.roll(v, ROPE_DIM // 2, 1))
        return v * cos2 + partner * sin2

    scale = HEAD_DIM ** -0.5 * LOG2_E
    pb = _dot(xn, w_ref[:, COL_DSWA:COL_DSWA + 3 * G])
    qkv_roped = (rope(pb[:, 0:G]) * scale, rope(pb[:, G:2 * G]), pb[:, 2 * G:3 * G])
    for part, val in enumerate(qkv_roped):
        for tile in range(G // LANES):
            qkvb_ref[part * (G // LANES) + tile] = val[:, tile * LANES:(tile + 1) * LANES]

    pc = _dot(xn, w_ref[:, COL_FOX:COL_FOX + 3 * G])
    qkc_ref[:, 0:G] = (pc[:, 0:G] * scale).astype(qkc_ref.dtype)
    qkc_ref[:, G:2 * G] = pc[:, G:2 * G].astype(qkc_ref.dtype)
    vtc_ref[...] = pc[:, 2 * G:3 * G].T.astype(vtc_ref.dtype)

    yd_ref[...] = (_dot(z, wpool_ref[...]) * pscale_ref[...]).astype(yd_ref.dtype)


def _mix_in(h, g, w_main, w_tail, cos_t, sin_t, wsc, wpool_bd, pscale, bf, pwin, seq, layer):
    n_tok = h.shape[0]
    tm = TOKEN_TILE
    G = GROUP_WIDTH
    row_spec = lambda width: pl.BlockSpec((tm, width), lambda i: (i, 0))
    slab_spec = lambda n: pl.BlockSpec((n, tm, LANES), lambda i: (0, i, 0))
    bf16 = jnp.bfloat16
    return pl.pallas_call(
        functools.partial(_mix_in_kernel, tiles_per_batch=seq // tm),
        grid=(n_tok // tm,),
        in_specs=[row_spec(D_MODEL), _layer_spec((1, D_MODEL), layer),
                  _layer_spec((D_MODEL, N_IN_MAIN), layer), _layer_spec((D_MODEL, N_IN_TAIL), layer),
                  row_spec(LANES), row_spec(LANES), _layer_spec((SUBLANES, G), layer),
                  _layer_spec((G, G), layer), _layer_spec((1, G), layer),
                  _layer_spec((1, LANES), layer), _const_spec((1, G))],
        out_specs=[row_spec(G), slab_spec(3 * G // LANES), row_spec(2 * G),
                   pl.BlockSpec((G, tm), lambda i: (0, i)), row_spec(G), row_spec(LANES),
                   pl.BlockSpec((SUBLANES, tm), lambda i: (0, i))],
        out_shape=[jax.ShapeDtypeStruct((n_tok, G), bf16),
                   jax.ShapeDtypeStruct((3 * G // LANES, n_tok, LANES), jnp.float32),
                   jax.ShapeDtypeStruct((n_tok, 2 * G), bf16),
                   jax.ShapeDtypeStruct((G, n_tok), bf16),
                   jax.ShapeDtypeStruct((n_tok, G), bf16),
                   jax.ShapeDtypeStruct((n_tok, LANES), jnp.float32),
                   jax.ShapeDtypeStruct((SUBLANES, n_tok), jnp.float32)],
        scratch_shapes=[pltpu.VMEM((SUBLANES, G), jnp.float32),
                        pltpu.VMEM((2 * SUBLANES, G), jnp.float32),
                        pltpu.VMEM((SUBLANES, LANES), jnp.float32)],
        compiler_params=_params(("arbitrary",)),
        name="mix_in",
    )(h, g, w_main, w_tail, cos_t, sin_t, wsc, wpool_bd, pscale, bf, pwin)


def _dswa_kernel(qkv_ref, out_ref, m_s, l_s, a_s):
    seq = qkv_ref.shape[1]
    QB = DSWA_BLOCK
    n_tiles = GROUP_WIDTH // LANES
    heads_per_tile = LANES // HEAD_DIM
    lane = lax.broadcasted_iota(jnp.int32, (1, LANES), 1)
    head_masks = [(lane // HEAD_DIM) == h for h in range(heads_per_tile)]
    qrow = lax.broadcasted_iota(jnp.int32, (QB, 2 * QB), 0)
    kcol = lax.broadcasted_iota(jnp.int32, (QB, 2 * QB), 1)
    band = (kcol >= qrow) & (kcol <= qrow + QB)
    ones_bd = jnp.concatenate(
        [jnp.broadcast_to(jnp.where(hm, 1.0, 0.0), (2 * QB, LANES)).astype(jnp.bfloat16)
         for hm in head_masks], axis=0)

    order = sorted(range(len(DSWA_CONFIGS)), key=lambda c: -DSWA_CONFIGS[c][1])
    for cfg in order:
        window, dil = DSWA_CONFIGS[cfg]
        assert window == QB * dil
        nblk = seq // (QB * dil)
        unroll = DSWA_UNROLL[cfg]
        assert nblk % unroll == 0
        trips_per_residue = nblk // unroll

        def blocks(it, carry, first=(cfg == order[0]), dil=dil, unroll=unroll,
                   trips_per_residue=trips_per_residue):
            if trips_per_residue == 1:
                rho, i0 = it, 0
            else:
                rho, i0 = it // trips_per_residue, (it % trips_per_residue) * unroll

            def rows_of(i):
                start = rho + (QB * dil) * i
                if dil > 1:
                    return pl.ds(start, QB, stride=dil)
                return pl.ds(pl.multiple_of(start, QB), QB)

            first_has_prev = (i0 > 0) if trips_per_residue > 1 else False
            block_rows = [rows_of(i0 + u) for u in range(unroll)]
            prev_rows = rows_of(jnp.maximum(i0 - 1, 0)) if trips_per_residue > 1 else None

            scored = []
            for tile in range(n_tiles):
                def key_value_tile(rows, tile=tile):
                    k = qkv_ref[n_tiles + tile, rows, :].astype(jnp.bfloat16)
                    v = qkv_ref[2 * n_tiles + tile, rows, :]
                    return k, [jnp.where(hm, v, 0.0).astype(jnp.bfloat16) for hm in head_masks]

                tiles = [key_value_tile(r) for r in block_rows]
                lead = key_value_tile(prev_rows) if prev_rows is not None else tiles[0]
                for u in range(unroll):
                    (k_prev, v_prev), (k_cur, v_cur) = (lead if u == 0 else tiles[u - 1]), tiles[u]
                    valid = band if u > 0 else band & ((kcol >= QB) | first_has_prev)
                    q = qkv_ref[tile, block_rows[u], :]
                    kb = jnp.concatenate([k_prev, k_cur], axis=0)
                    s_heads = [_dot_nt(jnp.where(hm, q, 0.0).astype(jnp.bfloat16), kb)
                               for hm in head_masks]
                    v_bd = jnp.concatenate(
                        [part for h in range(heads_per_tile) for part in (v_prev[h], v_cur[h])], axis=0)
                    scored.append((block_rows[u], valid, tile, s_heads, v_bd))

            weighted = []
            for rows, valid, tile, s_heads, v_bd in scored:
                probs, maxes = [], []
                for s in s_heads:
                    s = jnp.where(valid, s, NEG_INF)
                    m = jnp.max(s, axis=1, keepdims=True)
                    probs.append(jnp.exp2(s - m).astype(jnp.bfloat16))
                    maxes.append(m)
                m_full = jnp.where(head_masks[0], maxes[0], maxes[1])
                weighted.append((rows, tile, jnp.concatenate(probs, axis=1), m_full,
                                 jnp.concatenate([v_bd, ones_bd], axis=1)))

            reduced = [(rows, tile, m_full, _dot(p_all, v_aug))
                       for rows, tile, p_all, m_full, v_aug in weighted]

            for rows, tile, m_full, acc_l in reduced:
                acc, l_full = acc_l[:, 0:LANES], acc_l[:, LANES:2 * LANES]
                if first:
                    m_s[tile, rows, :] = m_full
                    l_s[tile, rows, :] = l_full
                    a_s[tile, rows, :] = acc
                else:
                    m_old = m_s[tile, rows, :]
                    m_new = jnp.maximum(m_old, m_full)
                    w_old = jnp.exp2(m_old - m_new)
                    w_cur = jnp.exp2(m_full - m_new)
                    m_s[tile, rows, :] = m_new
                    l_s[tile, rows, :] = l_s[tile, rows, :] * w_old + l_full * w_cur
                    a_s[tile, rows, :] = a_s[tile, rows, :] * w_old + acc * w_cur
            return carry

        lax.fori_loop(0, seq // (QB * unroll), blocks, 0)

    chunk = TOKEN_TILE
    for tile in range(n_tiles):
        for r in range(0, seq, chunk):
            out_ref[0, r:r + chunk, tile * LANES:(tile + 1) * LANES] = (
                a_s[tile, r:r + chunk, :] / l_s[tile, r:r + chunk, :]).astype(out_ref.dtype)


def _dswa(qkvb, batch, seq):
    G = GROUP_WIDTH
    n_slabs = 3 * G // LANES
    return pl.pallas_call(
        _dswa_kernel,
        grid=(batch,),
        in_specs=[pl.BlockSpec((n_slabs, seq, LANES), lambda b: (0, b, 0))],
        out_specs=pl.BlockSpec((1, seq, G), lambda b: (b, 0, 0)),
        out_shape=jax.ShapeDtypeStruct((batch, seq, G), jnp.bfloat16),
        scratch_shapes=[pltpu.VMEM((G // LANES, seq, LANES), jnp.float32)] * 3,
        compiler_params=_params(("arbitrary",)),
        name="dswa",
    )(qkvb)


def _fox_kernel(q_ref, k_ref, vt_ref, c_ref, ct_ref, out_ref):
    G = GROUP_WIDTH
    BQ = FOX_BLOCK
    H = N_GROUP_HEADS
    qi = pl.program_id(1)
    q = q_ref[0].astype(jnp.float32)
    lane = lax.broadcasted_iota(jnp.int32, (1, G), 1)
    q_heads = [jnp.where((lane // HEAD_DIM) == h, q, 0.0).astype(jnp.bfloat16) for h in range(H)]
    c_q = [ct_ref[h:h + 1, :] for h in range(H)]
    krow = lax.broadcasted_iota(jnp.int32, (BQ, BQ), 0)
    qcol = lax.broadcasted_iota(jnp.int32, (BQ, BQ), 1)
    causal = krow <= qcol

    ones_rows = jnp.ones((BF16_SUBLANES, BQ), jnp.bfloat16)

    def kv_step(blocks, carry):
        starts = [pl.multiple_of(j * BQ, BQ) for j, _ in blocks]
        keys = [k_ref[0, pl.ds(k0, BQ), :] for k0 in starts]
        c_keys = [c_ref[0, pl.ds(k0, BQ), :] for k0 in starts]
        chains = [(b, h) for b in range(len(blocks)) for h in range(H)]

        def scores(b, h):
            return _dot_nt(keys[b], q_heads[h]) - c_keys[b][:, h:h + 1]

        pending = [scores(*chains[n]) for n in range(min(FOX_LOOKAHEAD, len(chains)))]
        state = list(carry)
        for n, (b, h) in enumerate(chains):
            m, acc = state[h]
            s = pending.pop(0)
            if n + FOX_LOOKAHEAD < len(chains):
                pending.append(scores(*chains[n + FOX_LOOKAHEAD]))
            if blocks[b][1]:
                s = jnp.where(causal, s, NEG_INF)
            m_new = jnp.maximum(m, jnp.max(s, axis=0, keepdims=True) + c_q[h])
            p = jnp.exp2(s + (c_q[h] - m_new)).astype(jnp.bfloat16)
            v_t = jnp.concatenate(
                [vt_ref[h * HEAD_DIM:(h + 1) * HEAD_DIM, pl.ds(starts[b], BQ)], ones_rows], axis=0)
            state[h] = (m_new, jnp.exp2(m - m_new) * acc + _dot(v_t, p))
        return tuple(state)

    init = tuple((jnp.full((1, BQ), NEG_INF, jnp.float32),
                  jnp.zeros((HEAD_DIM + BF16_SUBLANES, BQ), jnp.float32)) for _ in range(H))
    carry = lax.fori_loop(
        0, qi // 2, lambda t, c: kv_step([(2 * t, False), (2 * t + 1, False)], c), init)
    final = lax.cond(qi % 2 == 1,
                     lambda c: kv_step([(qi - 1, False), (qi, True)], c),
                     lambda c: kv_step([(qi, True)], c), carry)
    out_t = jnp.concatenate([acc[0:HEAD_DIM] / acc[HEAD_DIM:HEAD_DIM + 1] for _, acc in final],
                            axis=0)
    out_ref[0] = out_t.T.astype(out_ref.dtype)


def _fox(qk, vt, c, ct, batch, seq):
    G = GROUP_WIDTH
    BQ = FOX_BLOCK
    nq = seq // BQ
    return pl.pallas_call(
        _fox_kernel,
        grid=(batch, nq),
        in_specs=[pl.BlockSpec((1, BQ, G), lambda b, i: (b, i, 0)),
                  pl.BlockSpec((1, seq, G), lambda b, i: (b, 0, 1)),
                  pl.BlockSpec((G, seq), lambda b, i: (0, b)),
                  pl.BlockSpec((1, seq, LANES), lambda b, i: (b, 0, 0)),
                  pl.BlockSpec((SUBLANES, BQ), lambda b, i: (0, b * nq + i))],
        out_specs=pl.BlockSpec((1, BQ, G), lambda b, i: (b, i, 0)),
        out_shape=jax.ShapeDtypeStruct((batch, seq, G), jnp.bfloat16),
        compiler_params=_params(("arbitrary", "arbitrary")),
        name="fox",
    )(qk.reshape(batch, seq, 2 * G), qk.reshape(batch, seq, 2 * G), vt, c.reshape(batch, seq, LANES), ct)


def _mem_kv_kernel(mem_ref, g_ref, w_ref, kv_ref):
    memn = _rms_norm(mem_ref[0], g_ref[...]).astype(jnp.bfloat16)
    kv = _dot(memn, w_ref[...])
    kv_ref[0, :, 0:D_MODEL] = (kv[:, 0:D_MODEL] * (XA_HEAD_DIM ** -0.5 * LOG2_E)).astype(kv_ref.dtype)
    kv_ref[0, :, D_MODEL:] = kv[:, D_MODEL:].astype(kv_ref.dtype)


def _mem_kv(mem, g, w_xkv, layer):
    batch, mem_len, _ = mem.shape
    return pl.pallas_call(
        _mem_kv_kernel,
        grid=(batch,),
        in_specs=[pl.BlockSpec((1, mem_len, D_MODEL), lambda b: (b, 0, 0)),
                  _layer_spec((1, D_MODEL), layer), _layer_spec((D_MODEL, 2 * D_MODEL), layer)],
        out_specs=pl.BlockSpec((1, mem_len, 2 * D_MODEL), lambda b: (b, 0, 0)),
        out_shape=jax.ShapeDtypeStruct((batch, mem_len, 2 * D_MODEL), jnp.bfloat16),
        compiler_params=_params(("arbitrary",)),
        name="mem_kv",
    )(mem, g, w_xkv)


def _mix_out_kernel(h_ref, ya_ref, yb_ref, yc_ref, yd_ref, wout_ref, g_ref, wq_ref, kv_ref, wo_ref,
                    out_ref):
    y = jnp.concatenate([ya_ref[...], yb_ref[...], yc_ref[...], yd_ref[...]], axis=1)
    h1 = h_ref[...] + _dot(y, wout_ref[...])
    q = _dot(_rms_norm(h1, g_ref[...]).astype(jnp.bfloat16), wq_ref[...])
    head_cols = [slice(hh * XA_HEAD_DIM, (hh + 1) * XA_HEAD_DIM) for hh in range(XA_HEADS)]
    scores = [_dot_nt(q[:, cols].astype(jnp.bfloat16), kv_ref[0, :, cols]) for cols in head_cols]
    probs = [jnp.exp2(s - jnp.max(s, axis=1, keepdims=True)) for s in scores]
    heads = []
    for cols, p in zip(head_cols, probs):
        o = _dot(p.astype(jnp.bfloat16), kv_ref[0, :, D_MODEL + cols.start:D_MODEL + cols.stop])
        heads.append((o / jnp.sum(p, axis=1, keepdims=True)).astype(jnp.bfloat16))
    out_ref[...] = h1 + _dot(jnp.concatenate(heads, axis=1), wo_ref[...])


def _mix_out(h, ya, yb, yc, yd, w_out, g_xa, w_xq, kv, w_xo, seq, layer):
    n_tok = h.shape[0]
    tm = TOKEN_TILE
    G = GROUP_WIDTH
    mem_len = kv.shape[1]
    tiles_per_batch = seq // tm
    row_spec = lambda width: pl.BlockSpec((tm, width), lambda i: (i, 0))
    return pl.pallas_call(
        _mix_out_kernel,
        grid=(n_tok // tm,),
        in_specs=[row_spec(D_MODEL), row_spec(G), row_spec(G), row_spec(G), row_spec(G),
                  _layer_spec((D_MODEL, D_MODEL), layer), _layer_spec((1, D_MODEL), layer),
                  _layer_spec((D_MODEL, D_MODEL), layer),
                  pl.BlockSpec((1, mem_len, 2 * D_MODEL), lambda i: (i // tiles_per_batch, 0, 0)),
                  _layer_spec((D_MODEL, D_MODEL), layer)],
        out_specs=row_spec(D_MODEL),
        out_shape=jax.ShapeDtypeStruct((n_tok, D_MODEL), jnp.float32),
        compiler_params=_params(("arbitrary",)),
        name="mix_out",
    )(h, ya, yb, yc, yd, w_out, g_xa, w_xq, kv, w_xo)


def _ffn_kernel(h_ref, g_ref, wup_ref, wconv_ref, wdown_ref, gfin_ref, out_ref, act_s, halo_s,
                *, tiles_per_batch, final_norm):
    tm = h_ref.shape[0]
    t = pl.program_id(0) % tiles_per_batch

    @pl.when(t == 0)
    def _():
        halo_s[...] = jnp.zeros_like(halo_s)

    h = h_ref[...]
    xn = _rms_norm(h, g_ref[...]).astype(jnp.bfloat16)

    def conv_branch(col):
        cols = slice(col, col + FFN_CHUNK)
        u = _dot(xn, wup_ref[:, cols])
        ext = jnp.concatenate([halo_s[:, cols], u], axis=0)
        y = (wconv_ref[2:3, cols] * ext + wconv_ref[1:2, cols] * _shift_rows(ext, 1)
             + wconv_ref[0:1, cols] * _shift_rows(ext, 2))
        halo_s[:, cols] = u[tm - SUBLANES:, :]
        return y[SUBLANES:, :]

    for col in range(0, D_FF, FFN_CHUNK):
        a = conv_branch(col)
        gate = conv_branch(D_FF + col)
        act_s[:, col:col + FFN_CHUNK] = (a * (gate * jax.nn.sigmoid(gate))).astype(act_s.dtype)

    out = h + _dot(act_s[...], wdown_ref[...])
    if final_norm:
        out = _rms_norm(out, gfin_ref[...])
    out_ref[...] = out


def _ffn(h, g_ffn, w_up, w_conv, w_down, g_final, seq, layer, final_norm):
    n_tok = h.shape[0]
    tm = TOKEN_TILE
    row_spec = pl.BlockSpec((tm, D_MODEL), lambda i: (i, 0))
    return pl.pallas_call(
        functools.partial(_ffn_kernel, tiles_per_batch=seq // tm, final_norm=final_norm),
        grid=(n_tok // tm,),
        in_specs=[row_spec, _layer_spec((1, D_MODEL), layer), _layer_spec((D_MODEL, 2 * D_FF), layer),
                  _layer_spec((SUBLANES, 2 * D_FF), layer), _layer_spec((D_FF, D_MODEL), layer),
                  _const_spec((1, D_MODEL))],
        out_specs=row_spec,
        out_shape=jax.ShapeDtypeStruct((n_tok, D_MODEL), jnp.float32),
        scratch_shapes=[pltpu.VMEM((tm, D_FF), jnp.bfloat16),
                        pltpu.VMEM((SUBLANES, 2 * D_FF), jnp.float32)],
        compiler_params=_params(("arbitrary",)),
        name="ffn",
    )(h, g_ffn, w_up, w_conv, w_down, g_final)


def _pad_axis(w, size, axis):
    pad = [(0, 0)] * w.ndim
    pad[axis] = (0, size - w.shape[axis])
    return jnp.pad(w, pad)


def kernel(x, mem, positions, g_mix, w_in, b_forget, w_sconv, w_pool, pool_scale, w_out, g_xa, g_mem,
           w_xq, w_xkv, w_xo, g_ffn, w_up, w_ffconv, w_down, g_final):
    batch, seq, _ = x.shape
    depth = w_in.shape[0]
    G = GROUP_WIDTH
    bf16 = jnp.bfloat16
    n_tok = batch * seq

    row = lambda p: p.reshape(depth, 1, p.shape[-1])
    w_main = w_in.astype(bf16)
    w_tail = _pad_axis(jnp.concatenate([w_in[:, :, N_IN_MAIN + N_GROUP_HEADS:],
                                        w_in[:, :, N_IN_MAIN:N_IN_MAIN + N_GROUP_HEADS]], axis=2),
                       N_IN_TAIL, 2).astype(bf16)
    bf = row(_pad_axis(b_forget, LANES, 1))
    n_groups = len(POOL_WINDOWS)
    wpool_bd = (w_pool[:, :, :, None, :] * jnp.eye(n_groups, dtype=w_pool.dtype)[None, :, None, :, None]
                ).reshape(depth, G, G).astype(bf16)
    wsc = _pad_axis(w_sconv, SUBLANES, 1)
    wfc = _pad_axis(w_ffconv, SUBLANES, 1)
    w_out_b, w_xq_b, w_xkv_b, w_xo_b = (w.astype(bf16) for w in (w_out, w_xq, w_xkv, w_xo))
    w_up_b, w_down_b = w_up.astype(bf16), w_down.astype(bf16)
    pwin = jnp.repeat(jnp.asarray(POOL_WINDOWS, jnp.int32), POOL_GROUP).reshape(1, G)

    cos_t, sin_t = _rope_tables(positions)
    h = x.reshape(n_tok, D_MODEL)
    for l in range(depth):
        ya, qkvb, qkc, vtc, yd, c, ct = _mix_in(
            h, row(g_mix), w_main, w_tail, cos_t, sin_t, wsc, wpool_bd, row(pool_scale), bf, pwin, seq, l)
        yb = _dswa(qkvb, batch, seq).reshape(n_tok, G)
        yc = _fox(qkc, vtc, c, ct, batch, seq).reshape(n_tok, G)
        kv = _mem_kv(mem, row(g_mem), w_xkv_b, l)
        h = _mix_out(h, ya, yb, yc, yd, w_out_b, row(g_xa), w_xq_b, kv, w_xo_b, seq, l)
        h = _ffn(h, row(g_ffn), w_up_b, wfc, w_down_b, g_final.reshape(1, D_MODEL), seq, l,
                 final_norm=(l == depth - 1))
    return h.reshape(batch, seq, D_MODEL)
```

```python
import functools

import jax
import jax.numpy as jnp
from jax import lax
from jax.experimental import pallas as pl
from jax.experimental.pallas import tpu as pltpu

D_MODEL = 1024
HEAD_DIM = 64
GROUP_WIDTH = D_MODEL // 4
N_GROUP_HEADS = GROUP_WIDTH // HEAD_DIM
DSWA_CONFIGS = ((128, 1), (512, 4), (2048, 16))
POOL_WINDOWS = (2, 4, 8, 16)
POOL_GROUP = GROUP_WIDTH // len(POOL_WINDOWS)
ROPE_THETA = 500000.0
ROPE_DIM = HEAD_DIM // 4
XA_HEADS = 4
XA_HEAD_DIM = D_MODEL // XA_HEADS
D_FF = ((8 * D_MODEL // 3 + 127) // 128) * 128
RMS_EPS = 1e-6
NEG_INF = -1e30
LOG2_E = 1.4426950408889634

LANES = 128
SUBLANES = 8
BF16_SUBLANES = 16
MXU_DIM = 256
VMEM_LIMIT_BYTES = 56 * 1024 * 1024

COL_CONV = 0
COL_DSWA = 3 * GROUP_WIDTH
COL_FOX = 6 * GROUP_WIDTH
N_IN_MAIN = 9 * GROUP_WIDTH
TAIL_POOL = 0
TAIL_FORGET = GROUP_WIDTH
N_IN_TAIL = TAIL_FORGET + LANES

TOKEN_TILE = 1024
DSWA_BLOCK = 128
DSWA_UNROLL = (4, 2, 2)
FOX_BLOCK = 512
FOX_LOOKAHEAD = 3
FFN_CHUNK = MXU_DIM

_NT = (((1,), (1,)), ((), ()))


def _dot(a, b):
    return jnp.dot(a, b, preferred_element_type=jnp.float32)


def _dot_nt(a, b):
    return lax.dot_general(a, b, _NT, preferred_element_type=jnp.float32)


def _rms_norm(x, g):
    return x * lax.rsqrt(jnp.mean(x * x, axis=-1, keepdims=True) + RMS_EPS) * g


def _params(semantics):
    return pltpu.CompilerParams(dimension_semantics=semantics, vmem_limit_bytes=VMEM_LIMIT_BYTES)


def _const_spec(shape):
    zeros = (0,) * len(shape)
    return pl.BlockSpec(shape, lambda *_: zeros)


def _layer_spec(shape, layer):
    index = (layer,) + (0,) * len(shape)
    return pl.BlockSpec((None,) + tuple(shape), lambda *_: index, pipeline_mode=pl.Buffered(1))


def _rope_table_kernel(pos_ref, freq_ref, spread_ref, cos_ref, sin_ref):
    ang = freq_ref[:, 0:1] * pos_ref[...].astype(jnp.float32)
    tn = (((0,), (0,)), ((), ()))

    def spread(a, w):
        out = None
        for _ in range(3):
            term = a.astype(jnp.bfloat16)
            a = a - term.astype(jnp.float32)
            part = lax.dot_general(term, w.astype(jnp.bfloat16), tn, preferred_element_type=jnp.float32)
            out = part if out is None else out + part
        return out

    cos_ref[...] = spread(jnp.cos(ang), spread_ref[0]) + spread_ref[2, 0:1, :]
    sin_ref[...] = spread(jnp.sin(ang), spread_ref[1])


def _rope_tables(positions):
    n_tok = positions.size
    half = ROPE_DIM // 2
    lane = jnp.arange(LANES) % HEAD_DIM
    inv_freq = ROPE_THETA ** (-jnp.arange(0, ROPE_DIM, 2, dtype=jnp.float32) / ROPE_DIM)
    freq = jnp.broadcast_to(inv_freq[:, None], (half, LANES))
    rotary = lane < ROPE_DIM
    one_hot = ((lane[None, :] % half) == jnp.arange(half)[:, None]) & rotary[None, :]
    sign = jnp.where(lane < half, -1.0, 1.0)
    spread = jnp.stack([one_hot.astype(jnp.float32), one_hot * sign[None, :],
                        jnp.broadcast_to(1.0 - rotary, (half, LANES))])
    tile = 2048
    table = jax.ShapeDtypeStruct((n_tok, LANES), jnp.float32)
    return pl.pallas_call(
        _rope_table_kernel,
        grid=(n_tok // tile,),
        in_specs=[pl.BlockSpec((1, tile), lambda i: (0, i)), _const_spec((half, LANES)),
                  _const_spec((3, half, LANES))],
        out_specs=[pl.BlockSpec((tile, LANES), lambda i: (i, 0))] * 2,
        out_shape=[table, table],
        compiler_params=_params(("arbitrary",)),
        name="rope_tables",
    )(positions.reshape(1, n_tok), freq, spread)


def _shift_rows(x, k):
    return pltpu.roll(x, k, 0)


def _mix_in_kernel(h_ref, g_ref, w_ref, wt_ref, cos_ref, sin_ref, wsc_ref, wpool_ref, pscale_ref,
                   bf_ref, pwin_ref, ya_ref, qkvb_ref, qkc_ref, vtc_ref, yd_ref, c_ref, ct_ref,
                   halo_a, halo_d, carry_c, *, tiles_per_batch):
    tm = h_ref.shape[0]
    G = GROUP_WIDTH
    t = pl.program_id(0) % tiles_per_batch

    @pl.when(t == 0)
    def _():
        halo_a[...] = jnp.zeros_like(halo_a)
        halo_d[...] = jnp.zeros_like(halo_d)
        carry_c[...] = jnp.zeros_like(carry_c)

    xn = _rms_norm(h_ref[...], g_ref[...]).astype(jnp.bfloat16)

    pa = _dot(xn, w_ref[:, COL_CONV:COL_CONV + 3 * G])
    x = pa[:, 2 * G:3 * G] * pa[:, 0:G]
    ext = jnp.concatenate([halo_a[...], x], axis=0)
    conv = (wsc_ref[2:3, :] * ext + wsc_ref[1:2, :] * _shift_rows(ext, 1)
            + wsc_ref[0:1, :] * _shift_rows(ext, 2))
    ya_ref[...] = (pa[:, G:2 * G] * conv[SUBLANES:, :]).astype(ya_ref.dtype)
    halo_a[...] = x[tm - SUBLANES:, :]

    pd = _dot(xn, wt_ref[:, TAIL_POOL:TAIL_POOL + G])
    halo_rows = halo_d.shape[0]
    extd = jnp.concatenate([halo_d[...], pd], axis=0)
    s2 = extd + _shift_rows(extd, 1)
    s4 = s2 + _shift_rows(s2, 2)
    s8 = s4 + _shift_rows(s4, 4)
    s16 = s8 + _shift_rows(s8, 8)
    lane = lax.broadcasted_iota(jnp.int32, (1, G), 1)
    wsum = jnp.where(lane < POOL_GROUP, s2,
                     jnp.where(lane < 2 * POOL_GROUP, s4,
                               jnp.where(lane < 3 * POOL_GROUP, s8, s16)))[halo_rows:, :]
    tpos = t * tm + lax.broadcasted_iota(jnp.int32, (tm, 1), 0)
    cnt = jnp.minimum(tpos + 1, pwin_ref[...]).astype(jnp.float32)
    z = (wsum / cnt - pd).astype(jnp.bfloat16)
    halo_d[...] = pd[tm - halo_rows:, :]

    pf = _dot(xn, wt_ref[:, TAIL_FORGET:TAIL_FORGET + LANES]) + bf_ref[...]
    c = (jnp.minimum(pf, 0.0) - jnp.log(1.0 + jnp.exp(-jnp.abs(pf)))) * LOG2_E
    row = lax.broadcasted_iota(jnp.int32, (tm, LANES), 0)
    k = 1
    while k < tm:
        c = c + jnp.where(row >= k, _shift_rows(c, k), 0.0)
        k *= 2
    c = c + carry_c[SUBLANES - 1:SUBLANES, :]
    c_ref[...] = c
    ct_ref[...] = c.T[0:SUBLANES, :]
    carry_c[...] = c[tm - SUBLANES:, :]

    cos2 = jnp.concatenate([cos_ref[...]] * (G // LANES), axis=1)
    sin2 = jnp.concatenate([sin_ref[...]] * (G // LANES), axis=1)
    first_half = (lane % HEAD_DIM) < (ROPE_DIM // 2)

    def rope(v):
        partner = jnp.where(first_half, pltpu.roll(v, G - ROPE_DIM // 2, 1),
                            pltpu.roll(v, ROPE_DIM // 2, 1))
        return v * cos2 + partner * sin2

    scale = HEAD_DIM ** -0.5 * LOG2_E
    pb = _dot(xn, w_ref[:, COL_DSWA:COL_DSWA + 3 * G])
    qkv_roped = (rope(pb[:, 0:G]) * scale, rope(pb[:, G:2 * G]), pb[:, 2 * G:3 * G])
    for part, val in enumerate(qkv_roped):
        for tile in range(G // LANES):
            qkvb_ref[part * (G // LANES) + tile] = val[:, tile * LANES:(tile + 1) * LANES]

    pc = _dot(xn, w_ref[:, COL_FOX:COL_FOX + 3 * G])
    qkc_ref[:, 0:G] = (pc[:, 0:G] * scale).astype(qkc_ref.dtype)
    qkc_ref[:, G:2 * G] = pc[:, G:2 * G].astype(qkc_ref.dtype)
    vtc_ref[...] = pc[:, 2 * G:3 * G].T.astype(vtc_ref.dtype)

    yd_ref[...] = (_dot(z, wpool_ref[...]) * pscale_ref[...]).astype(yd_ref.dtype)


def _mix_in(h, g, w_main, w_tail, cos_t, sin_t, wsc, wpool_bd, pscale, bf, pwin, seq, layer):
    n_tok = h.shape[0]
    tm = TOKEN_TILE
    G = GROUP_WIDTH
    row_spec = lambda width: pl.BlockSpec((tm, width), lambda i: (i, 0))
    slab_spec = lambda n: pl.BlockSpec((n, tm, LANES), lambda i: (0, i, 0))
    bf16 = jnp.bfloat16
    return pl.pallas_call(
        functools.partial(_mix_in_kernel, tiles_per_batch=seq // tm),
        grid=(n_tok // tm,),
        in_specs=[row_spec(D_MODEL), _layer_spec((1, D_MODEL), layer),
                  _layer_spec((D_MODEL, N_IN_MAIN), layer), _layer_spec((D_MODEL, N_IN_TAIL), layer),
                  row_spec(LANES), row_spec(LANES), _layer_spec((SUBLANES, G), layer),
                  _layer_spec((G, G), layer), _layer_spec((1, G), layer),
                  _layer_spec((1, LANES), layer), _const_spec((1, G))],
        out_specs=[row_spec(G), slab_spec(3 * G // LANES), row_spec(2 * G),
                   pl.BlockSpec((G, tm), lambda i: (0, i)), row_spec(G), row_spec(LANES),
                   pl.BlockSpec((SUBLANES, tm), lambda i: (0, i))],
        out_shape=[jax.ShapeDtypeStruct((n_tok, G), bf16),
                   jax.ShapeDtypeStruct((3 * G // LANES, n_tok, LANES), jnp.float32),
                   jax.ShapeDtypeStruct((n_tok, 2 * G), bf16),
                   jax.ShapeDtypeStruct((G, n_tok), bf16),
                   jax.ShapeDtypeStruct((n_tok, G), bf16),
                   jax.ShapeDtypeStruct((n_tok, LANES), jnp.float32),
                   jax.ShapeDtypeStruct((SUBLANES, n_tok), jnp.float32)],
        scratch_shapes=[pltpu.VMEM((SUBLANES, G), jnp.float32),
                        pltpu.VMEM((2 * SUBLANES, G), jnp.float32),
                        pltpu.VMEM((SUBLANES, LANES), jnp.float32)],
        compiler_params=_params(("arbitrary",)),
        name="mix_in",
    )(h, g, w_main, w_tail, cos_t, sin_t, wsc, wpool_bd, pscale, bf, pwin)


def _dswa_kernel(qkv_ref, out_ref, m_s, l_s, a_s):
    seq = qkv_ref.shape[1]
    QB = DSWA_BLOCK
    n_tiles = GROUP_WIDTH // LANES
    heads_per_tile = LANES // HEAD_DIM
    lane = lax.broadcasted_iota(jnp.int32, (1, LANES), 1)
    head_masks = [(lane // HEAD_DIM) == h for h in range(heads_per_tile)]
    qrow = lax.broadcasted_iota(jnp.int32, (QB, 2 * QB), 0)
    kcol = lax.broadcasted_iota(jnp.int32, (QB, 2 * QB), 1)
    band = (kcol >= qrow) & (kcol <= qrow + QB)
    ones_bd = jnp.concatenate(
        [jnp.broadcast_to(jnp.where(hm, 1.0, 0.0), (2 * QB, LANES)).astype(jnp.bfloat16)
         for hm in head_masks], axis=0)

    order = sorted(range(len(DSWA_CONFIGS)), key=lambda c: -DSWA_CONFIGS[c][1])
    for cfg in order:
        window, dil = DSWA_CONFIGS[cfg]
        assert window == QB * dil
        nblk = seq // (QB * dil)
        unroll = DSWA_UNROLL[cfg]
        assert nblk % unroll == 0
        trips_per_residue = nblk // unroll

        def blocks(it, carry, first=(cfg == order[0]), dil=dil, unroll=unroll,
                   trips_per_residue=trips_per_residue):
            if trips_per_residue == 1:
                rho, i0 = it, 0
            else:
                rho, i0 = it // trips_per_residue, (it % trips_per_residue) * unroll

            def rows_of(i):
                start = rho + (QB * dil) * i
                if dil > 1:
                    return pl.ds(start, QB, stride=dil)
                return pl.ds(pl.multiple_of(start, QB), QB)

            first_has_prev = (i0 > 0) if trips_per_residue > 1 else False
            block_rows = [rows_of(i0 + u) for u in range(unroll)]
            prev_rows = rows_of(jnp.maximum(i0 - 1, 0)) if trips_per_residue > 1 else None

            scored = []
            for tile in range(n_tiles):
                def key_value_tile(rows, tile=tile):
                    k = qkv_ref[n_tiles + tile, rows, :].astype(jnp.bfloat16)
                    v = qkv_ref[2 * n_tiles + tile, rows, :]
                    return k, [jnp.where(hm, v, 0.0).astype(jnp.bfloat16) for hm in head_masks]

                tiles = [key_value_tile(r) for r in block_rows]
                lead = key_value_tile(prev_rows) if prev_rows is not None else tiles[0]
                for u in range(unroll):
                    (k_prev, v_prev), (k_cur, v_cur) = (lead if u == 0 else tiles[u - 1]), tiles[u]
                    valid = band if u > 0 else band & ((kcol >= QB) | first_has_prev)
                    q = qkv_ref[tile, block_rows[u], :]
                    kb = jnp.concatenate([k_prev, k_cur], axis=0)
                    s_heads = [_dot_nt(jnp.where(hm, q, 0.0).astype(jnp.bfloat16), kb)
                               for hm in head_masks]
                    v_bd = jnp.concatenate(
                        [part for h in range(heads_per_tile) for part in (v_prev[h], v_cur[h])], axis=0)
                    scored.append((block_rows[u], valid, tile, s_heads, v_bd))

            weighted = []
            for rows, valid, tile, s_heads, v_bd in scored:
                probs, maxes = [], []
                for s in s_heads:
                    s = jnp.where(valid, s, NEG_INF)
                    m = jnp.max(s, axis=1, keepdims=True)
                    probs.append(jnp.exp2(s - m).astype(jnp.bfloat16))
                    maxes.append(m)
                m_full = jnp.where(head_masks[0], maxes[0], maxes[1])
                weighted.append((rows, tile, jnp.concatenate(probs, axis=1), m_full,
                                 jnp.concatenate([v_bd, ones_bd], axis=1)))

            reduced = [(rows, tile, m_full, _dot(p_all, v_aug))
                       for rows, tile, p_all, m_full, v_aug in weighted]

            for rows, tile, m_full, acc_l in reduced:
                acc, l_full = acc_l[:, 0:LANES], acc_l[:, LANES:2 * LANES]
                if first:
                    m_s[tile, rows, :] = m_full
                    l_s[tile, rows, :] = l_full
                    a_s[tile, rows, :] = acc
                else:
                    m_old = m_s[tile, rows, :]
                    m_new = jnp.maximum(m_old, m_full)
                    w_old = jnp.exp2(m_old - m_new)
                    w_cur = jnp.exp2(m_full - m_new)
                    m_s[tile, rows, :] = m_new
                    l_s[tile, rows, :] = l_s[tile, rows, :] * w_old + l_full * w_cur
                    a_s[tile, rows, :] = a_s[tile, rows, :] * w_old + acc * w_cur
            return carry

        lax.fori_loop(0, seq // (QB * unroll), blocks, 0)

    chunk = TOKEN_TILE
    for tile in range(n_tiles):
        for r in range(0, seq, chunk):
            out_ref[0, r:r + chunk, tile * LANES:(tile + 1) * LANES] = (
                a_s[tile, r:r + chunk, :] / l_s[tile, r:r + chunk, :]).astype(out_ref.dtype)


def _dswa(qkvb, batch, seq):
    G = GROUP_WIDTH
    n_slabs = 3 * G // LANES
    return pl.pallas_call(
        _dswa_kernel,
        grid=(batch,),
        in_specs=[pl.BlockSpec((n_slabs, seq, LANES), lambda b: (0, b, 0))],
        out_specs=pl.BlockSpec((1, seq, G), lambda b: (b, 0, 0)),
        out_shape=jax.ShapeDtypeStruct((batch, seq, G), jnp.bfloat16),
        scratch_shapes=[pltpu.VMEM((G // LANES, seq, LANES), jnp.float32)] * 3,
        compiler_params=_params(("arbitrary",)),
        name="dswa",
    )(qkvb)


def _fox_kernel(qk_ref, vt_ref, c_ref, ct_ref, out_ref):
    G = GROUP_WIDTH
    BQ = FOX_BLOCK
    H = N_GROUP_HEADS
    seq = qk_ref.shape[1]
    lane = lax.broadcasted_iota(jnp.int32, (1, G), 1)
    krow = lax.broadcasted_iota(jnp.int32, (BQ, BQ), 0)
    qcol = lax.broadcasted_iota(jnp.int32, (BQ, BQ), 1)
    causal = krow <= qcol
    ones_rows = jnp.ones((BF16_SUBLANES, BQ), jnp.bfloat16)

    def query_block(qi, _):
        q0 = pl.multiple_of(qi * BQ, BQ)
        q = qk_ref[0, pl.ds(q0, BQ), 0:G].astype(jnp.float32)
        q_heads = [jnp.where((lane // HEAD_DIM) == h, q, 0.0).astype(jnp.bfloat16) for h in range(H)]
        c_q = [ct_ref[h:h + 1, pl.ds(q0, BQ)] for h in range(H)]
        _fox_query_block(qi, q0, q_heads, c_q, causal, ones_rows, qk_ref, vt_ref, c_ref, out_ref)
        return 0

    lax.fori_loop(0, seq // BQ, query_block, 0)


def _fox_query_block(qi, q0, q_heads, c_q, causal, ones_rows, qk_ref, vt_ref, c_ref, out_ref):
    G = GROUP_WIDTH
    BQ = FOX_BLOCK
    H = N_GROUP_HEADS

    def kv_step(blocks, carry):
        starts = [pl.multiple_of(j * BQ, BQ) for j, _ in blocks]
        keys = [qk_ref[0, pl.ds(k0, BQ), G:2 * G] for k0 in starts]
        c_keys = [c_ref[0, pl.ds(k0, BQ), :] for k0 in starts]
        chains = [(b, h) for b in range(len(blocks)) for h in range(H)]

        def scores(b, h):
            return _dot_nt(keys[b], q_heads[h]) - c_keys[b][:, h:h + 1]

        pending = [scores(*chains[n]) for n in range(min(FOX_LOOKAHEAD, len(chains)))]
        state = list(carry)
        for n, (b, h) in enumerate(chains):
            m, acc = state[h]
            s = pending.pop(0)
            if n + FOX_LOOKAHEAD < len(chains):
                pending.append(scores(*chains[n + FOX_LOOKAHEAD]))
            if blocks[b][1]:
                s = jnp.where(causal, s, NEG_INF)
            m_new = jnp.maximum(m, jnp.max(s, axis=0, keepdims=True) + c_q[h])
            p = jnp.exp2(s + (c_q[h] - m_new)).astype(jnp.bfloat16)
            v_t = jnp.concatenate(
                [vt_ref[h * HEAD_DIM:(h + 1) * HEAD_DIM, pl.ds(starts[b], BQ)], ones_rows], axis=0)
            state[h] = (m_new, jnp.exp2(m - m_new) * acc + _dot(v_t, p))
        return tuple(state)

    init = tuple((jnp.full((1, BQ), NEG_INF, jnp.float32),
                  jnp.zeros((HEAD_DIM + BF16_SUBLANES, BQ), jnp.float32)) for _ in range(H))
    carry = lax.fori_loop(
        0, qi // 2, lambda t, c: kv_step([(2 * t, False), (2 * t + 1, False)], c), init)
    final = lax.cond(qi % 2 == 1,
                     lambda c: kv_step([(qi - 1, False), (qi, True)], c),
                     lambda c: kv_step([(qi, True)], c), carry)
    out_t = jnp.concatenate([acc[0:HEAD_DIM] / acc[HEAD_DIM:HEAD_DIM + 1] for _, acc in final],
                            axis=0)
    out_ref[0, pl.ds(q0, BQ), :] = out_t.T.astype(out_ref.dtype)


def _fox(qk, vt, c, ct, batch, seq):
    G = GROUP_WIDTH
    return pl.pallas_call(
        _fox_kernel,
        grid=(batch,),
        in_specs=[pl.BlockSpec((1, seq, 2 * G), lambda b: (b, 0, 0)),
                  pl.BlockSpec((G, seq), lambda b: (0, b)),
                  pl.BlockSpec((1, seq, LANES), lambda b: (b, 0, 0)),
                  pl.BlockSpec((SUBLANES, seq), lambda b: (0, b))],
        out_specs=pl.BlockSpec((1, seq, G), lambda b: (b, 0, 0)),
        out_shape=jax.ShapeDtypeStruct((batch, seq, G), jnp.bfloat16),
        compiler_params=_params(("arbitrary",)),
        name="fox",
    )(qk.reshape(batch, seq, 2 * G), vt, c.reshape(batch, seq, LANES), ct)


def _mem_kv_kernel(mem_ref, g_ref, w_ref, kv_ref):
    memn = _rms_norm(mem_ref[0], g_ref[...]).astype(jnp.bfloat16)
    kv = _dot(memn, w_ref[...])
    kv_ref[0, :, 0:D_MODEL] = (kv[:, 0:D_MODEL] * (XA_HEAD_DIM ** -0.5 * LOG2_E)).astype(kv_ref.dtype)
    kv_ref[0, :, D_MODEL:] = kv[:, D_MODEL:].astype(kv_ref.dtype)


def _mem_kv(mem, g, w_xkv, layer):
    batch, mem_len, _ = mem.shape
    return pl.pallas_call(
        _mem_kv_kernel,
        grid=(batch,),
        in_specs=[pl.BlockSpec((1, mem_len, D_MODEL), lambda b: (b, 0, 0)),
                  _layer_spec((1, D_MODEL), layer), _layer_spec((D_MODEL, 2 * D_MODEL), layer)],
        out_specs=pl.BlockSpec((1, mem_len, 2 * D_MODEL), lambda b: (b, 0, 0)),
        out_shape=jax.ShapeDtypeStruct((batch, mem_len, 2 * D_MODEL), jnp.bfloat16),
        compiler_params=_params(("arbitrary",)),
        name="mem_kv",
    )(mem, g, w_xkv)


def _mix_out_kernel(h_ref, ya_ref, yb_ref, yc_ref, yd_ref, wout_ref, g_ref, wq_ref, kv_ref, wo_ref,
                    out_ref):
    y = jnp.concatenate([ya_ref[...], yb_ref[...], yc_ref[...], yd_ref[...]], axis=1)
    h1 = h_ref[...] + _dot(y, wout_ref[...])
    q = _dot(_rms_norm(h1, g_ref[...]).astype(jnp.bfloat16), wq_ref[...])
    head_cols = [slice(hh * XA_HEAD_DIM, (hh + 1) * XA_HEAD_DIM) for hh in range(XA_HEADS)]
    scores = [_dot_nt(q[:, cols].astype(jnp.bfloat16), kv_ref[0, :, cols]) for cols in head_cols]
    probs = [jnp.exp2(s - jnp.max(s, axis=1, keepdims=True)) for s in scores]
    heads = []
    for cols, p in zip(head_cols, probs):
        o = _dot(p.astype(jnp.bfloat16), kv_ref[0, :, D_MODEL + cols.start:D_MODEL + cols.stop])
        heads.append((o / jnp.sum(p, axis=1, keepdims=True)).astype(jnp.bfloat16))
    out_ref[...] = h1 + _dot(jnp.concatenate(heads, axis=1), wo_ref[...])


def _mix_out(h, ya, yb, yc, yd, w_out, g_xa, w_xq, kv, w_xo, seq, layer):
    n_tok = h.shape[0]
    tm = TOKEN_TILE
    G = GROUP_WIDTH
    mem_len = kv.shape[1]
    tiles_per_batch = seq // tm
    row_spec = lambda width: pl.BlockSpec((tm, width), lambda i: (i, 0))
    return pl.pallas_call(
        _mix_out_kernel,
        grid=(n_tok // tm,),
        in_specs=[row_spec(D_MODEL), row_spec(G), row_spec(G), row_spec(G), row_spec(G),
                  _layer_spec((D_MODEL, D_MODEL), layer), _layer_spec((1, D_MODEL), layer),
                  _layer_spec((D_MODEL, D_MODEL), layer),
                  pl.BlockSpec((1, mem_len, 2 * D_MODEL), lambda i: (i // tiles_per_batch, 0, 0)),
                  _layer_spec((D_MODEL, D_MODEL), layer)],
        out_specs=row_spec(D_MODEL),
        out_shape=jax.ShapeDtypeStruct((n_tok, D_MODEL), jnp.float32),
        compiler_params=_params(("arbitrary",)),
        name="mix_out",
    )(h, ya, yb, yc, yd, w_out, g_xa, w_xq, kv, w_xo)


def _ffn_kernel(h_ref, g_ref, wup_ref, wconv_ref, wdown_ref, gfin_ref, out_ref, act_s, halo_s,
                *, tiles_per_batch, final_norm):
    tm = h_ref.shape[0]
    t = pl.program_id(0) % tiles_per_batch

    @pl.when(t == 0)
    def _():
        halo_s[...] = jnp.zeros_like(halo_s)

    h = h_ref[...]
    xn = _rms_norm(h, g_ref[...]).astype(jnp.bfloat16)

    def conv_branch(col):
        cols = slice(col, col + FFN_CHUNK)
        u = _dot(xn, wup_ref[:, cols])
        ext = jnp.concatenate([halo_s[:, cols], u], axis=0)
        y = (wconv_ref[2:3, cols] * ext + wconv_ref[1:2, cols] * _shift_rows(ext, 1)
             + wconv_ref[0:1, cols] * _shift_rows(ext, 2))
        halo_s[:, cols] = u[tm - SUBLANES:, :]
        return y[SUBLANES:, :]

    for col in range(0, D_FF, FFN_CHUNK):
        a = conv_branch(col)
        gate = conv_branch(D_FF + col)
        act_s[:, col:col + FFN_CHUNK] = (a * (gate * jax.nn.sigmoid(gate))).astype(act_s.dtype)

    out = h + _dot(act_s[...], wdown_ref[...])
    if final_norm:
        out = _rms_norm(out, gfin_ref[...])
    out_ref[...] = out


def _ffn(h, g_ffn, w_up, w_conv, w_down, g_final, seq, layer, final_norm):
    n_tok = h.shape[0]
    tm = TOKEN_TILE
    row_spec = pl.BlockSpec((tm, D_MODEL), lambda i: (i, 0))
    return pl.pallas_call(
        functools.partial(_ffn_kernel, tiles_per_batch=seq // tm, final_norm=final_norm),
        grid=(n_tok // tm,),
        in_specs=[row_spec, _layer_spec((1, D_MODEL), layer), _layer_spec((D_MODEL, 2 * D_FF), layer),
                  _layer_spec((SUBLANES, 2 * D_FF), layer), _layer_spec((D_FF, D_MODEL), layer),
                  _const_spec((1, D_MODEL))],
        out_specs=row_spec,
        out_shape=jax.ShapeDtypeStruct((n_tok, D_MODEL), jnp.float32),
        scratch_shapes=[pltpu.VMEM((tm, D_FF), jnp.bfloat16),
                        pltpu.VMEM((SUBLANES, 2 * D_FF), jnp.float32)],
        compiler_params=_params(("arbitrary",)),
        name="ffn",
    )(h, g_ffn, w_up, w_conv, w_down, g_final)


def _pad_axis(w, size, axis):
    pad = [(0, 0)] * w.ndim
    pad[axis] = (0, size - w.shape[axis])
    return jnp.pad(w, pad)


def kernel(x, mem, positions, g_mix, w_in, b_forget, w_sconv, w_pool, pool_scale, w_out, g_xa, g_mem,
           w_xq, w_xkv, w_xo, g_ffn, w_up, w_ffconv, w_down, g_final):
    batch, seq, _ = x.shape
    depth = w_in.shape[0]
    G = GROUP_WIDTH
    bf16 = jnp.bfloat16
    n_tok = batch * seq

    row = lambda p: p.reshape(depth, 1, p.shape[-1])
    w_main = _pad_axis(w_in, pl.cdiv(w_in.shape[2], LANES) * LANES, 2).astype(bf16)
    w_tail = _pad_axis(jnp.concatenate([w_in[:, :, N_IN_MAIN + N_GROUP_HEADS:],
                                        w_in[:, :, N_IN_MAIN:N_IN_MAIN + N_GROUP_HEADS]], axis=2),
                       N_IN_TAIL, 2).astype(bf16)
    bf = row(_pad_axis(b_forget, LANES, 1))
    n_groups = len(POOL_WINDOWS)
    wpool_bd = (w_pool[:, :, :, None, :] * jnp.eye(n_groups, dtype=w_pool.dtype)[None, :, None, :, None]
                ).reshape(depth, G, G).astype(bf16)
    wsc = _pad_axis(w_sconv, SUBLANES, 1)
    wfc = _pad_axis(w_ffconv, SUBLANES, 1)
    w_out_b, w_xq_b, w_xkv_b, w_xo_b = (w.astype(bf16) for w in (w_out, w_xq, w_xkv, w_xo))
    w_up_b, w_down_b = w_up.astype(bf16), w_down.astype(bf16)
    pwin = jnp.repeat(jnp.asarray(POOL_WINDOWS, jnp.int32), POOL_GROUP).reshape(1, G)

    cos_t, sin_t = _rope_tables(positions)
    h = x.reshape(n_tok, D_MODEL)
    for l in range(depth):
        ya, qkvb, qkc, vtc, yd, c, ct = _mix_in(
            h, row(g_mix), w_main, w_tail, cos_t, sin_t, wsc, wpool_bd, row(pool_scale), bf, pwin, seq, l)
        yb = _dswa(qkvb, batch, seq).reshape(n_tok, G)
        yc = _fox(qkc, vtc, c, ct, batch, seq).reshape(n_tok, G)
        kv = _mem_kv(mem, row(g_mem), w_xkv_b, l)
        h = _mix_out(h, ya, yb, yc, yd, w_out_b, row(g_xa), w_xq_b, kv, w_xo_b, seq, l)
        h = _ffn(h, row(g_ffn), w_up_b, wfc, w_down_b, g_final.reshape(1, D_MODEL), seq, l,
                 final_norm=(l == depth - 1))
    return h.reshape(batch, seq, D_MODEL)
```

```python
import functools

import jax
import jax.numpy as jnp
from jax import lax
from jax.experimental import pallas as pl
from jax.experimental.pallas import tpu as pltpu

D_MODEL = 1024
HEAD_DIM = 64
GROUP_WIDTH = D_MODEL // 4
N_GROUP_HEADS = GROUP_WIDTH // HEAD_DIM
DSWA_CONFIGS = ((128, 1), (512, 4), (2048, 16))
POOL_WINDOWS = (2, 4, 8, 16)
POOL_GROUP = GROUP_WIDTH // len(POOL_WINDOWS)
ROPE_THETA = 500000.0
ROPE_DIM = HEAD_DIM // 4
XA_HEADS = 4
XA_HEAD_DIM = D_MODEL // XA_HEADS
D_FF = ((8 * D_MODEL // 3 + 127) // 128) * 128
RMS_EPS = 1e-6
NEG_INF = -1e30
LOG2_E = 1.4426950408889634

LANES = 128
SUBLANES = 8
BF16_SUBLANES = 16
MXU_DIM = 256
VMEM_LIMIT_BYTES = 56 * 1024 * 1024

COL_CONV = 0
COL_DSWA = 3 * GROUP_WIDTH
COL_FOX = 6 * GROUP_WIDTH
N_IN_MAIN = 9 * GROUP_WIDTH
TAIL_POOL = 0
TAIL_FORGET = GROUP_WIDTH
N_IN_TAIL = TAIL_FORGET + LANES

TOKEN_TILE = 1024
DSWA_BLOCK = 128
DSWA_UNROLL = (4, 4, 2)
FOX_BLOCK = 512
FOX_LOOKAHEAD = 3
FFN_CHUNK = MXU_DIM

_NT = (((1,), (1,)), ((), ()))


def _dot(a, b):
    return jnp.dot(a, b, preferred_element_type=jnp.float32)


def _dot_nt(a, b):
    return lax.dot_general(a, b, _NT, preferred_element_type=jnp.float32)


def _rms_norm(x, g):
    return x * lax.rsqrt(jnp.mean(x * x, axis=-1, keepdims=True) + RMS_EPS) * g


def _params(semantics):
    return pltpu.CompilerParams(dimension_semantics=semantics, vmem_limit_bytes=VMEM_LIMIT_BYTES)


def _const_spec(shape):
    zeros = (0,) * len(shape)
    return pl.BlockSpec(shape, lambda *_: zeros)


def _layer_spec(shape, layer):
    index = (layer,) + (0,) * len(shape)
    return pl.BlockSpec((None,) + tuple(shape), lambda *_: index, pipeline_mode=pl.Buffered(1))


def _rope_table_kernel(pos_ref, freq_ref, spread_ref, cos_ref, sin_ref):
    ang = freq_ref[:, 0:1] * pos_ref[...].astype(jnp.float32)
    tn = (((0,), (0,)), ((), ()))

    def spread(a, w):
        out = None
        for _ in range(3):
            term = a.astype(jnp.bfloat16)
            a = a - term.astype(jnp.float32)
            part = lax.dot_general(term, w.astype(jnp.bfloat16), tn, preferred_element_type=jnp.float32)
            out = part if out is None else out + part
        return out

    cos_ref[...] = spread(jnp.cos(ang), spread_ref[0]) + spread_ref[2, 0:1, :]
    sin_ref[...] = spread(jnp.sin(ang), spread_ref[1])


def _rope_tables(positions):
    n_tok = positions.size
    half = ROPE_DIM // 2
    lane = jnp.arange(LANES) % HEAD_DIM
    inv_freq = ROPE_THETA ** (-jnp.arange(0, ROPE_DIM, 2, dtype=jnp.float32) / ROPE_DIM)
    freq = jnp.broadcast_to(inv_freq[:, None], (half, LANES))
    rotary = lane < ROPE_DIM
    one_hot = ((lane[None, :] % half) == jnp.arange(half)[:, None]) & rotary[None, :]
    sign = jnp.where(lane < half, -1.0, 1.0)
    spread = jnp.stack([one_hot.astype(jnp.float32), one_hot * sign[None, :],
                        jnp.broadcast_to(1.0 - rotary, (half, LANES))])
    tile = 2048
    table = jax.ShapeDtypeStruct((n_tok, LANES), jnp.float32)
    return pl.pallas_call(
        _rope_table_kernel,
        grid=(n_tok // tile,),
        in_specs=[pl.BlockSpec((1, tile), lambda i: (0, i)), _const_spec((half, LANES)),
                  _const_spec((3, half, LANES))],
        out_specs=[pl.BlockSpec((tile, LANES), lambda i: (i, 0))] * 2,
        out_shape=[table, table],
        compiler_params=_params(("arbitrary",)),
        name="rope_tables",
    )(positions.reshape(1, n_tok), freq, spread)


def _shift_rows(x, k):
    return pltpu.roll(x, k, 0)


def _mix_in_kernel(h_ref, g_ref, w_ref, wt_ref, cos_ref, sin_ref, wsc_ref, wpool_ref, pscale_ref,
                   bf_ref, pwin_ref, ya_ref, qkvb_ref, qkc_ref, vtc_ref, yd_ref, c_ref, ct_ref,
                   halo_a, halo_d, carry_c, *, tiles_per_batch):
    tm = h_ref.shape[0]
    G = GROUP_WIDTH
    t = pl.program_id(0) % tiles_per_batch

    @pl.when(t == 0)
    def _():
        halo_a[...] = jnp.zeros_like(halo_a)
        halo_d[...] = jnp.zeros_like(halo_d)
        carry_c[...] = jnp.zeros_like(carry_c)

    xn = _rms_norm(h_ref[...], g_ref[...]).astype(jnp.bfloat16)

    pa = _dot(xn, w_ref[:, COL_CONV:COL_CONV + 3 * G])
    x = pa[:, 2 * G:3 * G] * pa[:, 0:G]
    ext = jnp.concatenate([halo_a[...], x], axis=0)
    conv = (wsc_ref[2:3, :] * ext + wsc_ref[1:2, :] * _shift_rows(ext, 1)
            + wsc_ref[0:1, :] * _shift_rows(ext, 2))
    ya_ref[...] = (pa[:, G:2 * G] * conv[SUBLANES:, :]).astype(ya_ref.dtype)
    halo_a[...] = x[tm - SUBLANES:, :]

    pd = _dot(xn, wt_ref[:, TAIL_POOL:TAIL_POOL + G])
    halo_rows = halo_d.shape[0]
    extd = jnp.concatenate([halo_d[...], pd], axis=0)
    s2 = extd + _shift_rows(extd, 1)
    s4 = s2 + _shift_rows(s2, 2)
    s8 = s4 + _shift_rows(s4, 4)
    s16 = s8 + _shift_rows(s8, 8)
    lane = lax.broadcasted_iota(jnp.int32, (1, G), 1)
    wsum = jnp.where(lane < POOL_GROUP, s2,
                     jnp.where(lane < 2 * POOL_GROUP, s4,
                               jnp.where(lane < 3 * POOL_GROUP, s8, s16)))[halo_rows:, :]
    tpos = t * tm + lax.broadcasted_iota(jnp.int32, (tm, 1), 0)
    cnt = jnp.minimum(tpos + 1, pwin_ref[...]).astype(jnp.float32)
    z = (wsum / cnt - pd).astype(jnp.bfloat16)
    halo_d[...] = pd[tm - halo_rows:, :]

    pf = _dot(xn, wt_ref[:, TAIL_FORGET:TAIL_FORGET + LANES]) + bf_ref[...]
    c = (jnp.minimum(pf, 0.0) - jnp.log(1.0 + jnp.exp(-jnp.abs(pf)))) * LOG2_E
    row = lax.broadcasted_iota(jnp.int32, (tm, LANES), 0)
    k = 1
    while k < tm:
        c = c + jnp.where(row >= k, _shift_rows(c, k), 0.0)
        k *= 2
    c = c + carry_c[SUBLANES - 1:SUBLANES, :]
    c_ref[...] = c
    ct_ref[...] = c.T[0:SUBLANES, :]
    carry_c[...] = c[tm - SUBLANES:, :]

    cos2 = jnp.concatenate([cos_ref[...]] * (G // LANES), axis=1)
    sin2 = jnp.concatenate([sin_ref[...]] * (G // LANES), axis=1)
    first_half = (lane % HEAD_DIM) < (ROPE_DIM // 2)

    def rope(v):
        partner = jnp.where(first_half, pltpu.roll(v, G - ROPE_DIM // 2, 1),
                            pltpu.roll(v, ROPE_DIM // 2, 1))
        return v * cos2 + partner * sin2

    scale = HEAD_DIM ** -0.5 * LOG2_E
    pb = _dot(xn, w_ref[:, COL_DSWA:COL_DSWA + 3 * G])
    qkv_roped = (rope(pb[:, 0:G]) * scale, rope(pb[:, G:2 * G]), pb[:, 2 * G:3 * G])
    for part, val in enumerate(qkv_roped):
        for tile in range(G // LANES):
            qkvb_ref[part * (G // LANES) + tile] = val[:, tile * LANES:(tile + 1) * LANES]

    pc = _dot(xn, w_ref[:, COL_FOX:COL_FOX + 3 * G])
    qkc_ref[:, 0:G] = (pc[:, 0:G] * scale).astype(qkc_ref.dtype)
    qkc_ref[:, G:2 * G] = pc[:, G:2 * G].astype(qkc_ref.dtype)
    vtc_ref[...] = pc[:, 2 * G:3 * G].T.astype(vtc_ref.dtype)

    yd_ref[...] = (_dot(z, wpool_ref[...]) * pscale_ref[...]).astype(yd_ref.dtype)


def _mix_in(h, g, w_main, w_tail, cos_t, sin_t, wsc, wpool_bd, pscale, bf, pwin, seq, layer):
    n_tok = h.shape[0]
    tm = TOKEN_TILE
    G = GROUP_WIDTH
    row_spec = lambda width: pl.BlockSpec((tm, width), lambda i: (i, 0))
    slab_spec = lambda n: pl.BlockSpec((n, tm, LANES), lambda i: (0, i, 0))
    bf16 = jnp.bfloat16
    return pl.pallas_call(
        functools.partial(_mix_in_kernel, tiles_per_batch=seq // tm),
        grid=(n_tok // tm,),
        in_specs=[row_spec(D_MODEL), _layer_spec((1, D_MODEL), layer),
                  _layer_spec((D_MODEL, N_IN_MAIN), layer), _layer_spec((D_MODEL, N_IN_TAIL), layer),
                  row_spec(LANES), row_spec(LANES), _layer_spec((SUBLANES, G), layer),
                  _layer_spec((G, G), layer), _layer_spec((1, G), layer),
                  _layer_spec((1, LANES), layer), _const_spec((1, G))],
        out_specs=[row_spec(G), slab_spec(3 * G // LANES), row_spec(2 * G),
                   pl.BlockSpec((G, tm), lambda i: (0, i)), row_spec(G), row_spec(LANES),
                   pl.BlockSpec((SUBLANES, tm), lambda i: (0, i))],
        out_shape=[jax.ShapeDtypeStruct((n_tok, G), bf16),
                   jax.ShapeDtypeStruct((3 * G // LANES, n_tok, LANES), jnp.float32),
                   jax.ShapeDtypeStruct((n_tok, 2 * G), bf16),
                   jax.ShapeDtypeStruct((G, n_tok), bf16),
                   jax.ShapeDtypeStruct((n_tok, G), bf16),
                   jax.ShapeDtypeStruct((n_tok, LANES), jnp.float32),
                   jax.ShapeDtypeStruct((SUBLANES, n_tok), jnp.float32)],
        scratch_shapes=[pltpu.VMEM((SUBLANES, G), jnp.float32),
                        pltpu.VMEM((2 * SUBLANES, G), jnp.float32),
                        pltpu.VMEM((SUBLANES, LANES), jnp.float32)],
        compiler_params=_params(("arbitrary",)),
        name="mix_in",
    )(h, g, w_main, w_tail, cos_t, sin_t, wsc, wpool_bd, pscale, bf, pwin)


def _dswa_kernel(qkv_ref, out_ref, m_s, l_s, a_s):
    P = max(d for _, d in DSWA_CONFIGS)
    seq = qkv_ref.shape[1] * P
    QB = DSWA_BLOCK
    n_tiles = GROUP_WIDTH // LANES
    heads_per_tile = LANES // HEAD_DIM
    lane = lax.broadcasted_iota(jnp.int32, (1, LANES), 1)
    head_masks = [(lane // HEAD_DIM) == h for h in range(heads_per_tile)]
    qrow = lax.broadcasted_iota(jnp.int32, (QB, 2 * QB), 0)
    kcol = lax.broadcasted_iota(jnp.int32, (QB, 2 * QB), 1)
    ones_bd = jnp.concatenate(
        [jnp.broadcast_to(jnp.where(hm, 1.0, 0.0), (2 * QB, LANES)).astype(jnp.bfloat16)
         for hm in head_masks], axis=0)

    def phase_lanes(phase):
        if isinstance(phase, int):
            return slice(phase * LANES, (phase + 1) * LANES)
        return pl.ds(pl.multiple_of(phase * LANES, LANES), LANES)

    order = sorted(range(len(DSWA_CONFIGS)), key=lambda c: -DSWA_CONFIGS[c][1])
    for cfg in order:
        window, dil = DSWA_CONFIGS[cfg]
        assert window == QB * dil and P % dil == 0
        nblk = seq // (QB * dil)
        unroll = DSWA_UNROLL[cfg]
        assert nblk % unroll == 0
        trips_per_residue = nblk // unroll
        n_chunks = P // dil
        chunk_rows = QB // n_chunks
        in_block = lambda x, n_chunks=n_chunks, chunk_rows=chunk_rows: (
            n_chunks * (x % chunk_rows) + x // chunk_rows)
        delta = QB * (1 - kcol // QB) + in_block(qrow) - in_block(kcol % QB)
        band = (delta >= 0) & (delta <= QB)

        def blocks(it, carry, first=(cfg == order[0]), dil=dil, unroll=unroll, band=band,
                   trips_per_residue=trips_per_residue, n_chunks=n_chunks, chunk_rows=chunk_rows):
            if trips_per_residue == 1:
                rho, i0 = it, 0
            else:
                rho, i0 = it // trips_per_residue, (it % trips_per_residue) * unroll
            if dil == 1:
                rho = 0

            def chunks_of(i):
                rows = pl.ds(pl.multiple_of(i * chunk_rows, chunk_rows), chunk_rows)
                return [(rows, phase_lanes(rho + dil * m)) for m in range(n_chunks)]

            def load(ref, lead, chunks):
                parts = [ref[lead, rows, lanes] for rows, lanes in chunks]
                return parts[0] if len(parts) == 1 else jnp.concatenate(parts, axis=0)

            def store(ref, lead, chunks, val):
                for m, (rows, lanes) in enumerate(chunks):
                    ref[lead, rows, lanes] = val[m * chunk_rows:(m + 1) * chunk_rows]

            first_has_prev = (i0 > 0) if trips_per_residue > 1 else False
            block_rows = [chunks_of(i0 + u) for u in range(unroll)]
            prev_rows = chunks_of(jnp.maximum(i0 - 1, 0)) if trips_per_residue > 1 else None

            scored = []
            for tile in range(n_tiles):
                def key_value_tile(chunks, tile=tile):
                    k = load(qkv_ref, n_tiles + tile, chunks).astype(jnp.bfloat16)
                    v = load(qkv_ref, 2 * n_tiles + tile, chunks)
                    return k, [jnp.where(hm, v, 0.0).astype(jnp.bfloat16) for hm in head_masks]

                tiles = [key_value_tile(r) for r in block_rows]
                lead = key_value_tile(prev_rows) if prev_rows is not None else tiles[0]
                for u in range(unroll):
                    (k_prev, v_prev), (k_cur, v_cur) = (lead if u == 0 else tiles[u - 1]), tiles[u]
                    valid = band if u > 0 else band & ((kcol >= QB) | first_has_prev)
                    q = load(qkv_ref, tile, block_rows[u])
                    kb = jnp.concatenate([k_prev, k_cur], axis=0)
                    s_heads = [_dot_nt(jnp.where(hm, q, 0.0).astype(jnp.bfloat16), kb)
                               for hm in head_masks]
                    v_bd = jnp.concatenate(
                        [part for h in range(heads_per_tile) for part in (v_prev[h], v_cur[h])], axis=0)
                    scored.append((block_rows[u], valid, tile, s_heads, v_bd))

            weighted = []
            for rows, valid, tile, s_heads, v_bd in scored:
                probs, maxes = [], []
                for s in s_heads:
                    s = jnp.where(valid, s, NEG_INF)
                    m = jnp.max(s, axis=1, keepdims=True)
                    probs.append(jnp.exp2(s - m).astype(jnp.bfloat16))
                    maxes.append(m)
                m_full = jnp.where(head_masks[0], maxes[0], maxes[1])
                weighted.append((rows, tile, jnp.concatenate(probs, axis=1), m_full,
                                 jnp.concatenate([v_bd, ones_bd], axis=1)))

            reduced = [(rows, tile, m_full, _dot(p_all, v_aug))
                       for rows, tile, p_all, m_full, v_aug in weighted]

            for chunks, tile, m_full, acc_l in reduced:
                acc, l_full = acc_l[:, 0:LANES], acc_l[:, LANES:2 * LANES]
                if first:
                    store(m_s, tile, chunks, m_full)
                    store(l_s, tile, chunks, l_full)
                    store(a_s, tile, chunks, acc)
                else:
                    m_old = load(m_s, tile, chunks)
                    m_new = jnp.maximum(m_old, m_full)
                    w_old = jnp.exp2(m_old - m_new)
                    w_cur = jnp.exp2(m_full - m_new)
                    store(m_s, tile, chunks, m_new)
                    store(l_s, tile, chunks, load(l_s, tile, chunks) * w_old + l_full * w_cur)
                    store(a_s, tile, chunks, load(a_s, tile, chunks) * w_old + acc * w_cur)
            return carry

        lax.fori_loop(0, seq // (QB * unroll), blocks, 0)

    G = GROUP_WIDTH
    for phase in range(P):
        for tile in range(n_tiles):
            lanes = phase_lanes(phase)
            out_lanes = slice(phase * G + tile * LANES, phase * G + (tile + 1) * LANES)
            out_ref[0, :, out_lanes] = (a_s[tile, :, lanes] / l_s[tile, :, lanes]).astype(out_ref.dtype)


def _dswa(qkvb, batch, seq):
    G = GROUP_WIDTH
    n_slabs = 3 * G // LANES
    P = max(d for _, d in DSWA_CONFIGS)
    rows = seq // P
    out = pl.pallas_call(
        _dswa_kernel,
        grid=(batch,),
        in_specs=[pl.BlockSpec((n_slabs, rows, P * LANES), lambda b: (0, b, 0))],
        out_specs=pl.BlockSpec((1, rows, P * G), lambda b: (b, 0, 0)),
        out_shape=jax.ShapeDtypeStruct((batch, rows, P * G), jnp.bfloat16),
        scratch_shapes=[pltpu.VMEM((G // LANES, rows, P * LANES), jnp.float32)] * 3,
        compiler_params=_params(("arbitrary",)),
        name="dswa",
    )(qkvb.reshape(n_slabs, batch * rows, P * LANES))
    return out.reshape(batch, seq, G)


def _fox_kernel(qk_ref, vt_ref, c_ref, ct_ref, out_ref):
    G = GROUP_WIDTH
    BQ = FOX_BLOCK
    H = N_GROUP_HEADS
    seq = qk_ref.shape[1]
    lane = lax.broadcasted_iota(jnp.int32, (1, G), 1)
    krow = lax.broadcasted_iota(jnp.int32, (BQ, BQ), 0)
    qcol = lax.broadcasted_iota(jnp.int32, (BQ, BQ), 1)
    causal = krow <= qcol
    ones_rows = jnp.ones((BF16_SUBLANES, BQ), jnp.bfloat16)

    def query_block(qi, _):
        q0 = pl.multiple_of(qi * BQ, BQ)
        q = qk_ref[0, pl.ds(q0, BQ), 0:G].astype(jnp.float32)
        q_heads = [jnp.where((lane // HEAD_DIM) == h, q, 0.0).astype(jnp.bfloat16) for h in range(H)]
        c_q = [ct_ref[h:h + 1, pl.ds(q0, BQ)] for h in range(H)]
        _fox_query_block(qi, q0, q_heads, c_q, causal, ones_rows, qk_ref, vt_ref, c_ref, out_ref)
        return 0

    lax.fori_loop(0, seq // BQ, query_block, 0)


def _fox_query_block(qi, q0, q_heads, c_q, causal, ones_rows, qk_ref, vt_ref, c_ref, out_ref):
    G = GROUP_WIDTH
    BQ = FOX_BLOCK
    H = N_GROUP_HEADS

    def kv_step(blocks, carry):
        starts = [pl.multiple_of(j * BQ, BQ) for j, _ in blocks]
        keys = [qk_ref[0, pl.ds(k0, BQ), G:2 * G] for k0 in starts]
        c_keys = [c_ref[0, pl.ds(k0, BQ), :] for k0 in starts]
        chains = [(b, h) for b in range(len(blocks)) for h in range(H)]

        def scores(b, h):
            return _dot_nt(keys[b], q_heads[h]) - c_keys[b][:, h:h + 1]

        pending = [scores(*chains[n]) for n in range(min(FOX_LOOKAHEAD, len(chains)))]
        state = list(carry)
        for n, (b, h) in enumerate(chains):
            m, acc = state[h]
            s = pending.pop(0)
            if n + FOX_LOOKAHEAD < len(chains):
                pending.append(scores(*chains[n + FOX_LOOKAHEAD]))
            if blocks[b][1]:
                s = jnp.where(causal, s, NEG_INF)
            m_new = jnp.maximum(m, jnp.max(s, axis=0, keepdims=True) + c_q[h])
            p = jnp.exp2(s + (c_q[h] - m_new)).astype(jnp.bfloat16)
            v_t = jnp.concatenate(
                [vt_ref[h * HEAD_DIM:(h + 1) * HEAD_DIM, pl.ds(starts[b], BQ)], ones_rows], axis=0)
            state[h] = (m_new, jnp.exp2(m - m_new) * acc + _dot(v_t, p))
        return tuple(state)

    init = tuple((jnp.full((1, BQ), NEG_INF, jnp.float32),
                  jnp.zeros((HEAD_DIM + BF16_SUBLANES, BQ), jnp.float32)) for _ in range(H))
    carry = lax.fori_loop(
        0, qi // 2, lambda t, c: kv_step([(2 * t, False), (2 * t + 1, False)], c), init)
    final = lax.cond(qi % 2 == 1,
                     lambda c: kv_step([(qi - 1, False), (qi, True)], c),
                     lambda c: kv_step([(qi, True)], c), carry)
    out_t = jnp.concatenate([acc[0:HEAD_DIM] / acc[HEAD_DIM:HEAD_DIM + 1] for _, acc in final],
                            axis=0)
    out_ref[0, pl.ds(q0, BQ), :] = out_t.T.astype(out_ref.dtype)


def _fox(qk, vt, c, ct, batch, seq):
    G = GROUP_WIDTH
    return pl.pallas_call(
        _fox_kernel,
        grid=(batch,),
        in_specs=[pl.BlockSpec((1, seq, 2 * G), lambda b: (b, 0, 0)),
                  pl.BlockSpec((G, seq), lambda b: (0, b)),
                  pl.BlockSpec((1, seq, LANES), lambda b: (b, 0, 0)),
                  pl.BlockSpec((SUBLANES, seq), lambda b: (0, b))],
        out_specs=pl.BlockSpec((1, seq, G), lambda b: (b, 0, 0)),
        out_shape=jax.ShapeDtypeStruct((batch, seq, G), jnp.bfloat16),
        compiler_params=_params(("arbitrary",)),
        name="fox",
    )(qk.reshape(batch, seq, 2 * G), vt, c.reshape(batch, seq, LANES), ct)


def _mem_kv_kernel(mem_ref, g_ref, w_ref, kv_ref):
    memn = _rms_norm(mem_ref[0], g_ref[...]).astype(jnp.bfloat16)
    kv = _dot(memn, w_ref[...])
    kv_ref[0, :, 0:D_MODEL] = (kv[:, 0:D_MODEL] * (XA_HEAD_DIM ** -0.5 * LOG2_E)).astype(kv_ref.dtype)
    kv_ref[0, :, D_MODEL:] = kv[:, D_MODEL:].astype(kv_ref.dtype)


def _mem_kv(mem, g, w_xkv, layer):
    batch, mem_len, _ = mem.shape
    return pl.pallas_call(
        _mem_kv_kernel,
        grid=(batch,),
        in_specs=[pl.BlockSpec((1, mem_len, D_MODEL), lambda b: (b, 0, 0)),
                  _layer_spec((1, D_MODEL), layer), _layer_spec((D_MODEL, 2 * D_MODEL), layer)],
        out_specs=pl.BlockSpec((1, mem_len, 2 * D_MODEL), lambda b: (b, 0, 0)),
        out_shape=jax.ShapeDtypeStruct((batch, mem_len, 2 * D_MODEL), jnp.bfloat16),
        compiler_params=_params(("arbitrary",)),
        name="mem_kv",
    )(mem, g, w_xkv)


def _mix_out_kernel(h_ref, ya_ref, yb_ref, yc_ref, yd_ref, wout_ref, g_ref, wq_ref, kv_ref, wo_ref,
                    out_ref):
    y = jnp.concatenate([ya_ref[...], yb_ref[...], yc_ref[...], yd_ref[...]], axis=1)
    h1 = h_ref[...] + _dot(y, wout_ref[...])
    q = _dot(_rms_norm(h1, g_ref[...]).astype(jnp.bfloat16), wq_ref[...])
    head_cols = [slice(hh * XA_HEAD_DIM, (hh + 1) * XA_HEAD_DIM) for hh in range(XA_HEADS)]
    scores = [_dot_nt(q[:, cols].astype(jnp.bfloat16), kv_ref[0, :, cols]) for cols in head_cols]
    probs = [jnp.exp2(s - jnp.max(s, axis=1, keepdims=True)) for s in scores]
    heads = []
    for cols, p in zip(head_cols, probs):
        o = _dot(p.astype(jnp.bfloat16), kv_ref[0, :, D_MODEL + cols.start:D_MODEL + cols.stop])
        heads.append((o / jnp.sum(p, axis=1, keepdims=True)).astype(jnp.bfloat16))
    out_ref[...] = h1 + _dot(jnp.concatenate(heads, axis=1), wo_ref[...])


def _mix_out(h, ya, yb, yc, yd, w_out, g_xa, w_xq, kv, w_xo, seq, layer):
    n_tok = h.shape[0]
    tm = TOKEN_TILE
    G = GROUP_WIDTH
    mem_len = kv.shape[1]
    tiles_per_batch = seq // tm
    row_spec = lambda width: pl.BlockSpec((tm, width), lambda i: (i, 0))
    return pl.pallas_call(
        _mix_out_kernel,
        grid=(n_tok // tm,),
        in_specs=[row_spec(D_MODEL), row_spec(G), row_spec(G), row_spec(G), row_spec(G),
                  _layer_spec((D_MODEL, D_MODEL), layer), _layer_spec((1, D_MODEL), layer),
                  _layer_spec((D_MODEL, D_MODEL), layer),
                  pl.BlockSpec((1, mem_len, 2 * D_MODEL), lambda i: (i // tiles_per_batch, 0, 0)),
                  _layer_spec((D_MODEL, D_MODEL), layer)],
        out_specs=row_spec(D_MODEL),
        out_shape=jax.ShapeDtypeStruct((n_tok, D_MODEL), jnp.float32),
        compiler_params=_params(("arbitrary",)),
        name="mix_out",
    )(h, ya, yb, yc, yd, w_out, g_xa, w_xq, kv, w_xo)


def _ffn_kernel(h_ref, g_ref, wup_ref, wconv_ref, wdown_ref, gfin_ref, out_ref, act_s, halo_s,
                *, tiles_per_batch, final_norm):
    tm = h_ref.shape[0]
    t = pl.program_id(0) % tiles_per_batch

    @pl.when(t == 0)
    def _():
        halo_s[...] = jnp.zeros_like(halo_s)

    h = h_ref[...]
    xn = _rms_norm(h, g_ref[...]).astype(jnp.bfloat16)

    def conv_branch(col):
        cols = slice(col, col + FFN_CHUNK)
        u = _dot(xn, wup_ref[:, cols])
        ext = jnp.concatenate([halo_s[:, cols], u], axis=0)
        y = (wconv_ref[2:3, cols] * ext + wconv_ref[1:2, cols] * _shift_rows(ext, 1)
             + wconv_ref[0:1, cols] * _shift_rows(ext, 2))
        halo_s[:, cols] = u[tm - SUBLANES:, :]
        return y[SUBLANES:, :]

    for col in range(0, D_FF, FFN_CHUNK):
        a = conv_branch(col)
        gate = conv_branch(D_FF + col)
        act_s[:, col:col + FFN_CHUNK] = (a * (gate * jax.nn.sigmoid(gate))).astype(act_s.dtype)

    out = h + _dot(act_s[...], wdown_ref[...])
    if final_norm:
        out = _rms_norm(out, gfin_ref[...])
    out_ref[...] = out


def _ffn(h, g_ffn, w_up, w_conv, w_down, g_final, seq, layer, final_norm):
    n_tok = h.shape[0]
    tm = TOKEN_TILE
    row_spec = pl.BlockSpec((tm, D_MODEL), lambda i: (i, 0))
    return pl.pallas_call(
        functools.partial(_ffn_kernel, tiles_per_batch=seq // tm, final_norm=final_norm),
        grid=(n_tok // tm,),
        in_specs=[row_spec, _layer_spec((1, D_MODEL), layer), _layer_spec((D_MODEL, 2 * D_FF), layer),
                  _layer_spec((SUBLANES, 2 * D_FF), layer), _layer_spec((D_FF, D_MODEL), layer),
                  _const_spec((1, D_MODEL))],
        out_specs=row_spec,
        out_shape=jax.ShapeDtypeStruct((n_tok, D_MODEL), jnp.float32),
        scratch_shapes=[pltpu.VMEM((tm, D_FF), jnp.bfloat16),
                        pltpu.VMEM((SUBLANES, 2 * D_FF), jnp.float32)],
        compiler_params=_params(("arbitrary",)),
        name="ffn",
    )(h, g_ffn, w_up, w_conv, w_down, g_final)


def _pad_axis(w, size, axis):
    pad = [(0, 0)] * w.ndim
    pad[axis] = (0, size - w.shape[axis])
    return jnp.pad(w, pad)


def kernel(x, mem, positions, g_mix, w_in, b_forget, w_sconv, w_pool, pool_scale, w_out, g_xa, g_mem,
           w_xq, w_xkv, w_xo, g_ffn, w_up, w_ffconv, w_down, g_final):
    batch, seq, _ = x.shape
    depth = w_in.shape[0]
    G = GROUP_WIDTH
    bf16 = jnp.bfloat16
    n_tok = batch * seq

    row = lambda p: p.reshape(depth, 1, p.shape[-1])
    w_main = _pad_axis(w_in, pl.cdiv(w_in.shape[2], LANES) * LANES, 2).astype(bf16)
    w_tail = _pad_axis(jnp.concatenate([w_in[:, :, N_IN_MAIN + N_GROUP_HEADS:],
                                        w_in[:, :, N_IN_MAIN:N_IN_MAIN + N_GROUP_HEADS]], axis=2),
                       N_IN_TAIL, 2).astype(bf16)
    bf = row(_pad_axis(b_forget, LANES, 1))
    n_groups = len(POOL_WINDOWS)
    wpool_bd = (w_pool[:, :, :, None, :] * jnp.eye(n_groups, dtype=w_pool.dtype)[None, :, None, :, None]
                ).reshape(depth, G, G).astype(bf16)
    wsc = _pad_axis(w_sconv, SUBLANES, 1)
    wfc = _pad_axis(w_ffconv, SUBLANES, 1)
    w_out_b, w_xq_b, w_xkv_b, w_xo_b = (w.astype(bf16) for w in (w_out, w_xq, w_xkv, w_xo))
    w_up_b, w_down_b = w_up.astype(bf16), w_down.astype(bf16)
    pwin = jnp.repeat(jnp.asarray(POOL_WINDOWS, jnp.int32), POOL_GROUP).reshape(1, G)

    cos_t, sin_t = _rope_tables(positions)
    h = x.reshape(n_tok, D_MODEL)
    for l in range(depth):
        ya, qkvb, qkc, vtc, yd, c, ct = _mix_in(
            h, row(g_mix), w_main, w_tail, cos_t, sin_t, wsc, wpool_bd, row(pool_scale), bf, pwin, seq, l)
        yb = _dswa(qkvb, batch, seq).reshape(n_tok, G)
        yc = _fox(qkc, vtc, c, ct, batch, seq).reshape(n_tok, G)
        kv = _mem_kv(mem, row(g_mem), w_xkv_b, l)
        h = _mix_out(h, ya, yb, yc, yd, w_out_b, row(g_xa), w_xq_b, kv, w_xo_b, seq, l)
        h = _ffn(h, row(g_ffn), w_up_b, wfc, w_down_b, g_final.reshape(1, D_MODEL), seq, l,
                 final_norm=(l == depth - 1))
    return h.reshape(batch, seq, D_MODEL)
```

```python
import functools

import jax
import jax.numpy as jnp
from jax import lax
from jax.experimental import pallas as pl
from jax.experimental.pallas import tpu as pltpu

D_MODEL = 1024
HEAD_DIM = 64
GROUP_WIDTH = D_MODEL // 4
N_GROUP_HEADS = GROUP_WIDTH // HEAD_DIM
DSWA_CONFIGS = ((128, 1), (512, 4), (2048, 16))
POOL_WINDOWS = (2, 4, 8, 16)
POOL_GROUP = GROUP_WIDTH // len(POOL_WINDOWS)
ROPE_THETA = 500000.0
ROPE_DIM = HEAD_DIM // 4
XA_HEADS = 4
XA_HEAD_DIM = D_MODEL // XA_HEADS
D_FF = ((8 * D_MODEL // 3 + 127) // 128) * 128
RMS_EPS = 1e-6
NEG_INF = -1e30
LOG2_E = 1.4426950408889634

LANES = 128
SUBLANES = 8
BF16_SUBLANES = 16
MXU_DIM = 256
VMEM_LIMIT_BYTES = 56 * 1024 * 1024

COL_CONV = 0
COL_DSWA = 3 * GROUP_WIDTH
COL_FOX = 6 * GROUP_WIDTH
N_IN_MAIN = 9 * GROUP_WIDTH
TAIL_POOL = 0
TAIL_FORGET = GROUP_WIDTH
N_IN_TAIL = TAIL_FORGET + LANES

TOKEN_TILE = 1024
DSWA_BLOCK = 128
DSWA_UNROLL = (4, 4, 4)
FOX_BLOCK = 512
FOX_LOOKAHEAD = 3
FFN_CHUNK = MXU_DIM

_NT = (((1,), (1,)), ((), ()))


def _dot(a, b):
    return jnp.dot(a, b, preferred_element_type=jnp.float32)


def _dot_nt(a, b):
    return lax.dot_general(a, b, _NT, preferred_element_type=jnp.float32)


def _rms_norm(x, g):
    return x * lax.rsqrt(jnp.mean(x * x, axis=-1, keepdims=True) + RMS_EPS) * g


def _params(semantics):
    return pltpu.CompilerParams(dimension_semantics=semantics, vmem_limit_bytes=VMEM_LIMIT_BYTES)


def _const_spec(shape):
    zeros = (0,) * len(shape)
    return pl.BlockSpec(shape, lambda *_: zeros)


def _layer_spec(shape, layer):
    index = (layer,) + (0,) * len(shape)
    return pl.BlockSpec((None,) + tuple(shape), lambda *_: index, pipeline_mode=pl.Buffered(1))


def _rope_table_kernel(pos_ref, freq_ref, spread_ref, cos_ref, sin_ref):
    ang = freq_ref[:, 0:1] * pos_ref[...].astype(jnp.float32)
    tn = (((0,), (0,)), ((), ()))

    def spread(a, w):
        out = None
        for _ in range(3):
            term = a.astype(jnp.bfloat16)
            a = a - term.astype(jnp.float32)
            part = lax.dot_general(term, w.astype(jnp.bfloat16), tn, preferred_element_type=jnp.float32)
            out = part if out is None else out + part
        return out

    cos_ref[...] = spread(jnp.cos(ang), spread_ref[0]) + spread_ref[2, 0:1, :]
    sin_ref[...] = spread(jnp.sin(ang), spread_ref[1])


def _rope_tables(positions):
    n_tok = positions.size
    half = ROPE_DIM // 2
    lane = jnp.arange(LANES) % HEAD_DIM
    inv_freq = ROPE_THETA ** (-jnp.arange(0, ROPE_DIM, 2, dtype=jnp.float32) / ROPE_DIM)
    freq = jnp.broadcast_to(inv_freq[:, None], (half, LANES))
    rotary = lane < ROPE_DIM
    one_hot = ((lane[None, :] % half) == jnp.arange(half)[:, None]) & rotary[None, :]
    sign = jnp.where(lane < half, -1.0, 1.0)
    spread = jnp.stack([one_hot.astype(jnp.float32), one_hot * sign[None, :],
                        jnp.broadcast_to(1.0 - rotary, (half, LANES))])
    tile = 2048
    table = jax.ShapeDtypeStruct((n_tok, LANES), jnp.float32)
    return pl.pallas_call(
        _rope_table_kernel,
        grid=(n_tok // tile,),
        in_specs=[pl.BlockSpec((1, tile), lambda i: (0, i)), _const_spec((half, LANES)),
                  _const_spec((3, half, LANES))],
        out_specs=[pl.BlockSpec((tile, LANES), lambda i: (i, 0))] * 2,
        out_shape=[table, table],
        compiler_params=_params(("arbitrary",)),
        name="rope_tables",
    )(positions.reshape(1, n_tok), freq, spread)


def _shift_rows(x, k):
    return pltpu.roll(x, k, 0)


def _mix_in_kernel(h_ref, g_ref, w_ref, wt_ref, cos_ref, sin_ref, wsc_ref, wpool_ref, pscale_ref,
                   bf_ref, pwin_ref, ya_ref, qkvb_ref, qkc_ref, vtc_ref, yd_ref, c_ref, ct_ref,
                   halo_a, halo_d, carry_c, *, tiles_per_batch):
    tm = h_ref.shape[0]
    G = GROUP_WIDTH
    t = pl.program_id(0) % tiles_per_batch

    @pl.when(t == 0)
    def _():
        halo_a[...] = jnp.zeros_like(halo_a)
        halo_d[...] = jnp.zeros_like(halo_d)
        carry_c[...] = jnp.zeros_like(carry_c)

    half = tm // 2
    xn_halves, pa_halves = [], []
    for r in (0, half):
        xn_halves.append(_rms_norm(h_ref[r:r + half, :], g_ref[...]).astype(jnp.bfloat16))
        pa_halves.append(_dot(xn_halves[-1], w_ref[:, COL_CONV:COL_CONV + 3 * G]))
    xn = jnp.concatenate(xn_halves, axis=0)

    pa = jnp.concatenate(pa_halves, axis=0)
    x = pa[:, 2 * G:3 * G] * pa[:, 0:G]
    ext = jnp.concatenate([halo_a[...], x], axis=0)
    conv = (wsc_ref[2:3, :] * ext + wsc_ref[1:2, :] * _shift_rows(ext, 1)
            + wsc_ref[0:1, :] * _shift_rows(ext, 2))
    ya_ref[...] = (pa[:, G:2 * G] * conv[SUBLANES:, :]).astype(ya_ref.dtype)
    halo_a[...] = x[tm - SUBLANES:, :]

    pd = _dot(xn, wt_ref[:, TAIL_POOL:TAIL_POOL + G])
    halo_rows = halo_d.shape[0]
    extd = jnp.concatenate([halo_d[...], pd], axis=0)
    s2 = extd + _shift_rows(extd, 1)
    s4 = s2 + _shift_rows(s2, 2)
    s8 = s4 + _shift_rows(s4, 4)
    s16 = s8 + _shift_rows(s8, 8)
    lane = lax.broadcasted_iota(jnp.int32, (1, G), 1)
    wsum = jnp.where(lane < POOL_GROUP, s2,
                     jnp.where(lane < 2 * POOL_GROUP, s4,
                               jnp.where(lane < 3 * POOL_GROUP, s8, s16)))[halo_rows:, :]
    tpos = t * tm + lax.broadcasted_iota(jnp.int32, (tm, 1), 0)
    cnt = jnp.minimum(tpos + 1, pwin_ref[...]).astype(jnp.float32)
    z = (wsum / cnt - pd).astype(jnp.bfloat16)
    halo_d[...] = pd[tm - halo_rows:, :]

    pf = _dot(xn, wt_ref[:, TAIL_FORGET:TAIL_FORGET + LANES]) + bf_ref[...]
    c = (jnp.minimum(pf, 0.0) - jnp.log(1.0 + jnp.exp(-jnp.abs(pf)))) * LOG2_E
    row = lax.broadcasted_iota(jnp.int32, (tm, LANES), 0)
    k = 1
    while k < tm:
        c = c + jnp.where(row >= k, _shift_rows(c, k), 0.0)
        k *= 2
    c = c + carry_c[SUBLANES - 1:SUBLANES, :]
    c_ref[...] = c
    ct_ref[...] = c.T[0:SUBLANES, :]
    carry_c[...] = c[tm - SUBLANES:, :]

    cos2 = jnp.concatenate([cos_ref[...]] * (G // LANES), axis=1)
    sin2 = jnp.concatenate([sin_ref[...]] * (G // LANES), axis=1)
    first_half = (lane % HEAD_DIM) < (ROPE_DIM // 2)

    def rope(v):
        partner = jnp.where(first_half, pltpu.roll(v, G - ROPE_DIM // 2, 1),
                            pltpu.roll(v, ROPE_DIM // 2, 1))
        return v * cos2 + partner * sin2

    scale = HEAD_DIM ** -0.5 * LOG2_E
    pb = _dot(xn, w_ref[:, COL_DSWA:COL_DSWA + 3 * G])
    qkv_roped = (rope(pb[:, 0:G]) * scale, rope(pb[:, G:2 * G]), pb[:, 2 * G:3 * G])
    for part, val in enumerate(qkv_roped):
        for tile in range(G // LANES):
            qkvb_ref[part * (G // LANES) + tile] = val[:, tile * LANES:(tile + 1) * LANES]

    pc = _dot(xn, w_ref[:, COL_FOX:COL_FOX + 3 * G])
    qkc_ref[:, 0:G] = (pc[:, 0:G] * scale).astype(qkc_ref.dtype)
    qkc_ref[:, G:2 * G] = pc[:, G:2 * G].astype(qkc_ref.dtype)
    vtc_ref[...] = pc[:, 2 * G:3 * G].T.astype(vtc_ref.dtype)

    yd_ref[...] = (_dot(z, wpool_ref[...]) * pscale_ref[...]).astype(yd_ref.dtype)


def _mix_in(h, g, w_main, w_tail, cos_t, sin_t, wsc, wpool_bd, pscale, bf, pwin, seq, layer):
    n_tok = h.shape[0]
    tm = TOKEN_TILE
    G = GROUP_WIDTH
    row_spec = lambda width: pl.BlockSpec((tm, width), lambda i: (i, 0))
    slab_spec = lambda n: pl.BlockSpec((n, tm, LANES), lambda i: (0, i, 0))
    bf16 = jnp.bfloat16
    return pl.pallas_call(
        functools.partial(_mix_in_kernel, tiles_per_batch=seq // tm),
        grid=(n_tok // tm,),
        in_specs=[row_spec(D_MODEL), _layer_spec((1, D_MODEL), layer),
                  _layer_spec((D_MODEL, N_IN_MAIN), layer), _layer_spec((D_MODEL, N_IN_TAIL), layer),
                  row_spec(LANES), row_spec(LANES), _layer_spec((SUBLANES, G), layer),
                  _layer_spec((G, G), layer), _layer_spec((1, G), layer),
                  _layer_spec((1, LANES), layer), _const_spec((1, G))],
        out_specs=[row_spec(G), slab_spec(3 * G // LANES), row_spec(2 * G),
                   pl.BlockSpec((G, tm), lambda i: (0, i)), row_spec(G), row_spec(LANES),
                   pl.BlockSpec((SUBLANES, tm), lambda i: (0, i))],
        out_shape=[jax.ShapeDtypeStruct((n_tok, G), bf16),
                   jax.ShapeDtypeStruct((3 * G // LANES, n_tok, LANES), jnp.float32),
                   jax.ShapeDtypeStruct((n_tok, 2 * G), bf16),
                   jax.ShapeDtypeStruct((G, n_tok), bf16),
                   jax.ShapeDtypeStruct((n_tok, G), bf16),
                   jax.ShapeDtypeStruct((n_tok, LANES), jnp.float32),
                   jax.ShapeDtypeStruct((SUBLANES, n_tok), jnp.float32)],
        scratch_shapes=[pltpu.VMEM((SUBLANES, G), jnp.float32),
                        pltpu.VMEM((2 * SUBLANES, G), jnp.float32),
                        pltpu.VMEM((SUBLANES, LANES), jnp.float32)],
        compiler_params=_params(("arbitrary",)),
        name="mix_in",
    )(h, g, w_main, w_tail, cos_t, sin_t, wsc, wpool_bd, pscale, bf, pwin)


def _dswa_kernel(qkv_ref, out_ref, m_s, l_s, a_s):
    seq = qkv_ref.shape[1]
    QB = DSWA_BLOCK
    n_tiles = GROUP_WIDTH // LANES
    heads_per_tile = LANES // HEAD_DIM
    lane = lax.broadcasted_iota(jnp.int32, (1, LANES), 1)
    head_masks = [(lane // HEAD_DIM) == h for h in range(heads_per_tile)]
    qrow = lax.broadcasted_iota(jnp.int32, (QB, 2 * QB), 0)
    kcol = lax.broadcasted_iota(jnp.int32, (QB, 2 * QB), 1)
    band = (kcol >= qrow) & (kcol <= qrow + QB)
    ones_bd = jnp.concatenate(
        [jnp.broadcast_to(jnp.where(hm, 1.0, 0.0), (2 * QB, LANES)).astype(jnp.bfloat16)
         for hm in head_masks], axis=0)

    order = sorted(range(len(DSWA_CONFIGS)), key=lambda c: -DSWA_CONFIGS[c][1])
    for cfg in order:
        window, dil = DSWA_CONFIGS[cfg]
        assert window == QB * dil
        nblk = seq // (QB * dil)
        unroll = DSWA_UNROLL[cfg]
        assert nblk % unroll == 0 or unroll % nblk == 0
        run = min(unroll, nblk)
        runs_per_trip = unroll // run
        trips_per_residue = nblk // run

        def blocks(it, carry, first=(cfg == order[0]), dil=dil, run=run, runs_per_trip=runs_per_trip,
                   trips_per_residue=trips_per_residue):
            scored = []
            for r in range(runs_per_trip):
                if trips_per_residue == 1:
                    rho, i0 = it * runs_per_trip + r, 0
                else:
                    rho, i0 = it // trips_per_residue, (it % trips_per_residue) * run

                def rows_of(i, rho=rho):
                    start = rho + (QB * dil) * i
                    if dil > 1:
                        return pl.ds(start, QB, stride=dil)
                    return pl.ds(pl.multiple_of(start, QB), QB)

                first_has_prev = (i0 > 0) if trips_per_residue > 1 else False
                block_rows = [rows_of(i0 + u) for u in range(run)]
                prev_rows = rows_of(jnp.maximum(i0 - 1, 0)) if trips_per_residue > 1 else None

                for tile in range(n_tiles):
                    def key_value_tile(rows, tile=tile):
                        k = qkv_ref[n_tiles + tile, rows, :].astype(jnp.bfloat16)
                        v = qkv_ref[2 * n_tiles + tile, rows, :]
                        return k, [jnp.where(hm, v, 0.0).astype(jnp.bfloat16) for hm in head_masks]

                    tiles = [key_value_tile(rows) for rows in block_rows]
                    lead = key_value_tile(prev_rows) if prev_rows is not None else tiles[0]
                    for u in range(run):
                        (k_prev, v_prev), (k_cur, v_cur) = (lead if u == 0 else tiles[u - 1]), tiles[u]
                        valid = band if u > 0 else band & ((kcol >= QB) | first_has_prev)
                        q = qkv_ref[tile, block_rows[u], :]
                        kb = jnp.concatenate([k_prev, k_cur], axis=0)
                        s_heads = [_dot_nt(jnp.where(hm, q, 0.0).astype(jnp.bfloat16), kb)
                                   for hm in head_masks]
                        v_bd = jnp.concatenate(
                            [part for h in range(heads_per_tile) for part in (v_prev[h], v_cur[h])],
                            axis=0)
                        scored.append((block_rows[u], valid, tile, s_heads, v_bd))

            weighted = []
            for rows, valid, tile, s_heads, v_bd in scored:
                probs, maxes = [], []
                for s in s_heads:
                    s = jnp.where(valid, s, NEG_INF)
                    m = jnp.max(s, axis=1, keepdims=True)
                    probs.append(jnp.exp2(s - m).astype(jnp.bfloat16))
                    maxes.append(m)
                m_full = jnp.where(head_masks[0], maxes[0], maxes[1])
                weighted.append((rows, tile, jnp.concatenate(probs, axis=1), m_full,
                                 jnp.concatenate([v_bd, ones_bd], axis=1)))

            reduced = [(rows, tile, m_full, _dot(p_all, v_aug))
                       for rows, tile, p_all, m_full, v_aug in weighted]

            for rows, tile, m_full, acc_l in reduced:
                acc, l_full = acc_l[:, 0:LANES], acc_l[:, LANES:2 * LANES]
                if first:
                    m_s[tile, rows, :] = m_full
                    l_s[tile, rows, :] = l_full
                    a_s[tile, rows, :] = acc
                else:
                    m_old = m_s[tile, rows, :]
                    m_new = jnp.maximum(m_old, m_full)
                    w_old = jnp.exp2(m_old - m_new)
                    w_cur = jnp.exp2(m_full - m_new)
                    m_s[tile, rows, :] = m_new
                    l_s[tile, rows, :] = l_s[tile, rows, :] * w_old + l_full * w_cur
                    a_s[tile, rows, :] = a_s[tile, rows, :] * w_old + acc * w_cur
            return carry

        lax.fori_loop(0, seq // (QB * unroll), blocks, 0)

    chunk = TOKEN_TILE
    for tile in range(n_tiles):
        for r in range(0, seq, chunk):
            out_ref[0, r:r + chunk, tile * LANES:(tile + 1) * LANES] = (
                a_s[tile, r:r + chunk, :] / l_s[tile, r:r + chunk, :]).astype(out_ref.dtype)


def _dswa(qkvb, batch, seq):
    G = GROUP_WIDTH
    n_slabs = 3 * G // LANES
    return pl.pallas_call(
        _dswa_kernel,
        grid=(batch,),
        in_specs=[pl.BlockSpec((n_slabs, seq, LANES), lambda b: (0, b, 0))],
        out_specs=pl.BlockSpec((1, seq, G), lambda b: (b, 0, 0)),
        out_shape=jax.ShapeDtypeStruct((batch, seq, G), jnp.bfloat16),
        scratch_shapes=[pltpu.VMEM((G // LANES, seq, LANES), jnp.float32)] * 3,
        compiler_params=_params(("arbitrary",)),
        name="dswa",
    )(qkvb)


def _fox_kernel(qk_ref, vt_ref, c_ref, ct_ref, out_ref):
    G = GROUP_WIDTH
    BQ = FOX_BLOCK
    H = N_GROUP_HEADS
    seq = qk_ref.shape[1]
    lane = lax.broadcasted_iota(jnp.int32, (1, G), 1)
    krow = lax.broadcasted_iota(jnp.int32, (BQ, BQ), 0)
    qcol = lax.broadcasted_iota(jnp.int32, (BQ, BQ), 1)
    causal = krow <= qcol
    ones_rows = jnp.ones((BF16_SUBLANES, BQ), jnp.bfloat16)

    def query_block(qi, _):
        q0 = pl.multiple_of(qi * BQ, BQ)
        q = qk_ref[0, pl.ds(q0, BQ), 0:G].astype(jnp.float32)
        q_heads = [jnp.where((lane // HEAD_DIM) == h, q, 0.0).astype(jnp.bfloat16) for h in range(H)]
        c_q = [ct_ref[h:h + 1, pl.ds(q0, BQ)] for h in range(H)]
        _fox_query_block(qi, q0, q_heads, c_q, causal, ones_rows, qk_ref, vt_ref, c_ref, out_ref)
        return 0

    lax.fori_loop(0, seq // BQ, query_block, 0)


def _fox_query_block(qi, q0, q_heads, c_q, causal, ones_rows, qk_ref, vt_ref, c_ref, out_ref):
    G = GROUP_WIDTH
    BQ = FOX_BLOCK
    H = N_GROUP_HEADS

    def kv_step(blocks, carry):
        starts = [pl.multiple_of(j * BQ, BQ) for j, _ in blocks]
        keys = [qk_ref[0, pl.ds(k0, BQ), G:2 * G] for k0 in starts]
        c_keys = [c_ref[0, pl.ds(k0, BQ), :] for k0 in starts]
        chains = [(b, h) for b in range(len(blocks)) for h in range(H)]

        def scores(b, h):
            return _dot_nt(keys[b], q_heads[h]) - c_keys[b][:, h:h + 1]

        pending = [scores(*chains[n]) for n in range(min(FOX_LOOKAHEAD, len(chains)))]
        state = list(carry)
        for n, (b, h) in enumerate(chains):
            m, acc = state[h]
            s = pending.pop(0)
            if n + FOX_LOOKAHEAD < len(chains):
                pending.append(scores(*chains[n + FOX_LOOKAHEAD]))
            if blocks[b][1]:
                s = jnp.where(causal, s, NEG_INF)
            m_new = jnp.maximum(m, jnp.max(s, axis=0, keepdims=True) + c_q[h])
            p = jnp.exp2(s + (c_q[h] - m_new)).astype(jnp.bfloat16)
            v_t = jnp.concatenate(
                [vt_ref[h * HEAD_DIM:(h + 1) * HEAD_DIM, pl.ds(starts[b], BQ)], ones_rows], axis=0)
            state[h] = (m_new, jnp.exp2(m - m_new) * acc + _dot(v_t, p))
        return tuple(state)

    init = tuple((jnp.full((1, BQ), NEG_INF, jnp.float32),
                  jnp.zeros((HEAD_DIM + BF16_SUBLANES, BQ), jnp.float32)) for _ in range(H))
    carry = lax.fori_loop(
        0, qi // 2, lambda t, c: kv_step([(2 * t, False), (2 * t + 1, False)], c), init)
    final = lax.cond(qi % 2 == 1,
                     lambda c: kv_step([(qi - 1, False), (qi, True)], c),
                     lambda c: kv_step([(qi, True)], c), carry)
    out_t = jnp.concatenate([acc[0:HEAD_DIM] / acc[HEAD_DIM:HEAD_DIM + 1] for _, acc in final],
                            axis=0)
    out_ref[0, pl.ds(q0, BQ), :] = out_t.T.astype(out_ref.dtype)


def _fox(qk, vt, c, ct, batch, seq):
    G = GROUP_WIDTH
    return pl.pallas_call(
        _fox_kernel,
        grid=(batch,),
        in_specs=[pl.BlockSpec((1, seq, 2 * G), lambda b: (b, 0, 0)),
                  pl.BlockSpec((G, seq), lambda b: (0, b)),
                  pl.BlockSpec((1, seq, LANES), lambda b: (b, 0, 0)),
                  pl.BlockSpec((SUBLANES, seq), lambda b: (0, b))],
        out_specs=pl.BlockSpec((1, seq, G), lambda b: (b, 0, 0)),
        out_shape=jax.ShapeDtypeStruct((batch, seq, G), jnp.bfloat16),
        compiler_params=_params(("arbitrary",)),
        name="fox",
    )(qk.reshape(batch, seq, 2 * G), vt, c.reshape(batch, seq, LANES), ct)


def _mem_kv_kernel(mem_ref, g_ref, w_ref, kv_ref):
    memn = _rms_norm(mem_ref[0], g_ref[...]).astype(jnp.bfloat16)
    kv = _dot(memn, w_ref[...])
    kv_ref[0, :, 0:D_MODEL] = (kv[:, 0:D_MODEL] * (XA_HEAD_DIM ** -0.5 * LOG2_E)).astype(kv_ref.dtype)
    kv_ref[0, :, D_MODEL:] = kv[:, D_MODEL:].astype(kv_ref.dtype)


def _mem_kv(mem, g, w_xkv, layer):
    batch, mem_len, _ = mem.shape
    return pl.pallas_call(
        _mem_kv_kernel,
        grid=(batch,),
        in_specs=[pl.BlockSpec((1, mem_len, D_MODEL), lambda b: (b, 0, 0)),
                  _layer_spec((1, D_MODEL), layer), _layer_spec((D_MODEL, 2 * D_MODEL), layer)],
        out_specs=pl.BlockSpec((1, mem_len, 2 * D_MODEL), lambda b: (b, 0, 0)),
        out_shape=jax.ShapeDtypeStruct((batch, mem_len, 2 * D_MODEL), jnp.bfloat16),
        compiler_params=_params(("arbitrary",)),
        name="mem_kv",
    )(mem, g, w_xkv)


def _mix_out_kernel(h_ref, ya_ref, yb_ref, yc_ref, yd_ref, wout_ref, g_ref, wq_ref, kv_ref, wo_ref,
                    out_ref):
    y = jnp.concatenate([ya_ref[...], yb_ref[...], yc_ref[...], yd_ref[...]], axis=1)
    h1 = h_ref[...] + _dot(y, wout_ref[...])
    q = _dot(_rms_norm(h1, g_ref[...]).astype(jnp.bfloat16), wq_ref[...])
    head_cols = [slice(hh * XA_HEAD_DIM, (hh + 1) * XA_HEAD_DIM) for hh in range(XA_HEADS)]
    scores = [_dot_nt(q[:, cols].astype(jnp.bfloat16), kv_ref[0, :, cols]) for cols in head_cols]
    probs = [jnp.exp2(s - jnp.max(s, axis=1, keepdims=True)) for s in scores]
    heads = []
    for cols, p in zip(head_cols, probs):
        o = _dot(p.astype(jnp.bfloat16), kv_ref[0, :, D_MODEL + cols.start:D_MODEL + cols.stop])
        heads.append((o / jnp.sum(p, axis=1, keepdims=True)).astype(jnp.bfloat16))
    out_ref[...] = h1 + _dot(jnp.concatenate(heads, axis=1), wo_ref[...])


def _mix_out(h, ya, yb, yc, yd, w_out, g_xa, w_xq, kv, w_xo, seq, layer):
    n_tok = h.shape[0]
    tm = TOKEN_TILE
    G = GROUP_WIDTH
    mem_len = kv.shape[1]
    tiles_per_batch = seq // tm
    row_spec = lambda width: pl.BlockSpec((tm, width), lambda i: (i, 0))
    return pl.pallas_call(
        _mix_out_kernel,
        grid=(n_tok // tm,),
        in_specs=[row_spec(D_MODEL), row_spec(G), row_spec(G), row_spec(G), row_spec(G),
                  _layer_spec((D_MODEL, D_MODEL), layer), _layer_spec((1, D_MODEL), layer),
                  _layer_spec((D_MODEL, D_MODEL), layer),
                  pl.BlockSpec((1, mem_len, 2 * D_MODEL), lambda i: (i // tiles_per_batch, 0, 0)),
                  _layer_spec((D_MODEL, D_MODEL), layer)],
        out_specs=row_spec(D_MODEL),
        out_shape=jax.ShapeDtypeStruct((n_tok, D_MODEL), jnp.float32),
        compiler_params=_params(("arbitrary",)),
        name="mix_out",
    )(h, ya, yb, yc, yd, w_out, g_xa, w_xq, kv, w_xo)


def _ffn_kernel(h_ref, g_ref, wup_ref, wconv_ref, wdown_ref, gfin_ref, out_ref, act_s, halo_s,
                *, tiles_per_batch, final_norm):
    tm = h_ref.shape[0]
    t = pl.program_id(0) % tiles_per_batch

    @pl.when(t == 0)
    def _():
        halo_s[...] = jnp.zeros_like(halo_s)

    h = h_ref[...]
    xn = _rms_norm(h, g_ref[...]).astype(jnp.bfloat16)

    def conv_branch(col):
        cols = slice(col, col + FFN_CHUNK)
        u = _dot(xn, wup_ref[:, cols])
        ext = jnp.concatenate([halo_s[:, cols], u], axis=0)
        y = (wconv_ref[2:3, cols] * ext + wconv_ref[1:2, cols] * _shift_rows(ext, 1)
             + wconv_ref[0:1, cols] * _shift_rows(ext, 2))
        halo_s[:, cols] = u[tm - SUBLANES:, :]
        return y[SUBLANES:, :]

    for col in range(0, D_FF, FFN_CHUNK):
        a = conv_branch(col)
        gate = conv_branch(D_FF + col)
        act_s[:, col:col + FFN_CHUNK] = (a * (gate * jax.nn.sigmoid(gate))).astype(act_s.dtype)

    out = h + _dot(act_s[...], wdown_ref[...])
    if final_norm:
        out = _rms_norm(out, gfin_ref[...])
    out_ref[...] = out


def _ffn(h, g_ffn, w_up, w_conv, w_down, g_final, seq, layer, final_norm):
    n_tok = h.shape[0]
    tm = TOKEN_TILE
    row_spec = pl.BlockSpec((tm, D_MODEL), lambda i: (i, 0))
    return pl.pallas_call(
        functools.partial(_ffn_kernel, tiles_per_batch=seq // tm, final_norm=final_norm),
        grid=(n_tok // tm,),
        in_specs=[row_spec, _layer_spec((1, D_MODEL), layer), _layer_spec((D_MODEL, 2 * D_FF), layer),
                  _layer_spec((SUBLANES, 2 * D_FF), layer), _layer_spec((D_FF, D_MODEL), layer),
                  _const_spec((1, D_MODEL))],
        out_specs=row_spec,
        out_shape=jax.ShapeDtypeStruct((n_tok, D_MODEL), jnp.float32),
        scratch_shapes=[pltpu.VMEM((tm, D_FF), jnp.bfloat16),
                        pltpu.VMEM((SUBLANES, 2 * D_FF), jnp.float32)],
        compiler_params=_params(("arbitrary",)),
        name="ffn",
    )(h, g_ffn, w_up, w_conv, w_down, g_final)


def _pad_axis(w, size, axis):
    pad = [(0, 0)] * w.ndim
    pad[axis] = (0, size - w.shape[axis])
    return jnp.pad(w, pad)


def kernel(x, mem, positions, g_mix, w_in, b_forget, w_sconv, w_pool, pool_scale, w_out, g_xa, g_mem,
           w_xq, w_xkv, w_xo, g_ffn, w_up, w_ffconv, w_down, g_final):
    batch, seq, _ = x.shape
    depth = w_in.shape[0]
    G = GROUP_WIDTH
    bf16 = jnp.bfloat16
    n_tok = batch * seq

    row = lambda p: p.reshape(depth, 1, p.shape[-1])
    w_main = _pad_axis(w_in, pl.cdiv(w_in.shape[2], LANES) * LANES, 2).astype(bf16)
    w_tail = _pad_axis(jnp.concatenate([w_in[:, :, N_IN_MAIN + N_GROUP_HEADS:],
                                        w_in[:, :, N_IN_MAIN:N_IN_MAIN + N_GROUP_HEADS]], axis=2),
                       N_IN_TAIL, 2).astype(bf16)
    bf = row(_pad_axis(b_forget, LANES, 1))
    n_groups = len(POOL_WINDOWS)
    wpool_bd = (w_pool[:, :, :, None, :] * jnp.eye(n_groups, dtype=w_pool.dtype)[None, :, None, :, None]
                ).reshape(depth, G, G).astype(bf16)
    wsc = _pad_axis(w_sconv, SUBLANES, 1)
    wfc = _pad_axis(w_ffconv, SUBLANES, 1)
    w_out_b, w_xq_b, w_xkv_b, w_xo_b = (w.astype(bf16) for w in (w_out, w_xq, w_xkv, w_xo))
    w_up_b, w_down_b = w_up.astype(bf16), w_down.astype(bf16)
    pwin = jnp.repeat(jnp.asarray(POOL_WINDOWS, jnp.int32), POOL_GROUP).reshape(1, G)

    cos_t, sin_t = _rope_tables(positions)
    h = x.reshape(n_tok, D_MODEL)
    for l in range(depth):
        ya, qkvb, qkc, vtc, yd, c, ct = _mix_in(
            h, row(g_mix), w_main, w_tail, cos_t, sin_t, wsc, wpool_bd, row(pool_scale), bf, pwin, seq, l)
        yb = _dswa(qkvb, batch, seq).reshape(n_tok, G)
        yc = _fox(qkc, vtc, c, ct, batch, seq).reshape(n_tok, G)
        kv = _mem_kv(mem, row(g_mem), w_xkv_b, l)
        h = _mix_out(h, ya, yb, yc, yd, w_out_b, row(g_xa), w_xq_b, kv, w_xo_b, seq, l)
        h = _ffn(h, row(g_ffn), w_up_b, wfc, w_down_b, g_final.reshape(1, D_MODEL), seq, l,
                 final_norm=(l == depth - 1))
    return h.reshape(batch, seq, D_MODEL)
```

```python
import functools

import jax
import jax.numpy as jnp
from jax import lax
from jax.experimental import pallas as pl
from jax.experimental.pallas import tpu as pltpu

D_MODEL = 1024
HEAD_DIM = 64
GROUP_WIDTH = D_MODEL // 4
N_GROUP_HEADS = GROUP_WIDTH // HEAD_DIM
DSWA_CONFIGS = ((128, 1), (512, 4), (2048, 16))
POOL_WINDOWS = (2, 4, 8, 16)
POOL_GROUP = GROUP_WIDTH // len(POOL_WINDOWS)
ROPE_THETA = 500000.0
ROPE_DIM = HEAD_DIM // 4
XA_HEADS = 4
XA_HEAD_DIM = D_MODEL // XA_HEADS
D_FF = ((8 * D_MODEL // 3 + 127) // 128) * 128
RMS_EPS = 1e-6
NEG_INF = -1e30
LOG2_E = 1.4426950408889634

LANES = 128
SUBLANES = 8
BF16_SUBLANES = 16
MXU_DIM = 256
VMEM_LIMIT_BYTES = 56 * 1024 * 1024

COL_CONV = 0
COL_DSWA = 3 * GROUP_WIDTH
COL_FOX = 6 * GROUP_WIDTH
N_IN_MAIN = 9 * GROUP_WIDTH
TAIL_POOL = 0
TAIL_FORGET = GROUP_WIDTH
N_IN_TAIL = TAIL_FORGET + LANES

TOKEN_TILE = 1024
DSWA_BLOCK = 128
DSWA_UNROLL = (8, 8, 8)
FOX_BLOCK = 512
FOX_LOOKAHEAD = 3
FFN_CHUNK = MXU_DIM

_NT = (((1,), (1,)), ((), ()))


def _dot(a, b):
    return jnp.dot(a, b, preferred_element_type=jnp.float32)


def _dot_nt(a, b):
    return lax.dot_general(a, b, _NT, preferred_element_type=jnp.float32)


def _rms_norm(x, g):
    return x * lax.rsqrt(jnp.mean(x * x, axis=-1, keepdims=True) + RMS_EPS) * g


def _params(semantics):
    return pltpu.CompilerParams(dimension_semantics=semantics, vmem_limit_bytes=VMEM_LIMIT_BYTES)


def _const_spec(shape):
    zeros = (0,) * len(shape)
    return pl.BlockSpec(shape, lambda *_: zeros)


def _layer_spec(shape, layer):
    index = (layer,) + (0,) * len(shape)
    return pl.BlockSpec((None,) + tuple(shape), lambda *_: index, pipeline_mode=pl.Buffered(1))


def _rope_table_kernel(pos_ref, freq_ref, spread_ref, cos_ref, sin_ref):
    ang = freq_ref[:, 0:1] * pos_ref[...].astype(jnp.float32)
    tn = (((0,), (0,)), ((), ()))

    def spread(a, w):
        out = None
        for _ in range(3):
            term = a.astype(jnp.bfloat16)
            a = a - term.astype(jnp.float32)
            part = lax.dot_general(term, w.astype(jnp.bfloat16), tn, preferred_element_type=jnp.float32)
            out = part if out is None else out + part
        return out

    cos_ref[...] = spread(jnp.cos(ang), spread_ref[0]) + spread_ref[2, 0:1, :]
    sin_ref[...] = spread(jnp.sin(ang), spread_ref[1])


def _rope_tables(positions):
    n_tok = positions.size
    half = ROPE_DIM // 2
    lane = jnp.arange(LANES) % HEAD_DIM
    inv_freq = ROPE_THETA ** (-jnp.arange(0, ROPE_DIM, 2, dtype=jnp.float32) / ROPE_DIM)
    freq = jnp.broadcast_to(inv_freq[:, None], (half, LANES))
    rotary = lane < ROPE_DIM
    one_hot = ((lane[None, :] % half) == jnp.arange(half)[:, None]) & rotary[None, :]
    sign = jnp.where(lane < half, -1.0, 1.0)
    spread = jnp.stack([one_hot.astype(jnp.float32), one_hot * sign[None, :],
                        jnp.broadcast_to(1.0 - rotary, (half, LANES))])
    tile = 2048
    table = jax.ShapeDtypeStruct((n_tok, LANES), jnp.float32)
    return pl.pallas_call(
        _rope_table_kernel,
        grid=(n_tok // tile,),
        in_specs=[pl.BlockSpec((1, tile), lambda i: (0, i)), _const_spec((half, LANES)),
                  _const_spec((3, half, LANES))],
        out_specs=[pl.BlockSpec((tile, LANES), lambda i: (i, 0))] * 2,
        out_shape=[table, table],
        compiler_params=_params(("arbitrary",)),
        name="rope_tables",
    )(positions.reshape(1, n_tok), freq, spread)


def _shift_rows(x, k):
    return pltpu.roll(x, k, 0)


def _mix_in_kernel(h_ref, g_ref, w_ref, wt_ref, cos_ref, sin_ref, wsc_ref, wpool_ref, pscale_ref,
                   bf_ref, pwin_ref, ya_ref, qkvb_ref, qkc_ref, vtc_ref, yd_ref, c_ref, ct_ref,
                   halo_a, halo_d, carry_c, *, tiles_per_batch):
    tm = h_ref.shape[0]
    G = GROUP_WIDTH
    t = pl.program_id(0) % tiles_per_batch

    @pl.when(t == 0)
    def _():
        halo_a[...] = jnp.zeros_like(halo_a)
        halo_d[...] = jnp.zeros_like(halo_d)
        carry_c[...] = jnp.zeros_like(carry_c)

    half = tm // 2
    xn_halves, pa_halves = [], []
    for r in (0, half):
        xn_halves.append(_rms_norm(h_ref[r:r + half, :], g_ref[...]).astype(jnp.bfloat16))
        pa_halves.append(_dot(xn_halves[-1], w_ref[:, COL_CONV:COL_CONV + 3 * G]))
    xn = jnp.concatenate(xn_halves, axis=0)

    pa = jnp.concatenate(pa_halves, axis=0)
    x = pa[:, 2 * G:3 * G] * pa[:, 0:G]
    ext = jnp.concatenate([halo_a[...], x], axis=0)
    conv = (wsc_ref[2:3, :] * ext + wsc_ref[1:2, :] * _shift_rows(ext, 1)
            + wsc_ref[0:1, :] * _shift_rows(ext, 2))
    ya_ref[...] = (pa[:, G:2 * G] * conv[SUBLANES:, :]).astype(ya_ref.dtype)
    halo_a[...] = x[tm - SUBLANES:, :]

    pd = _dot(xn, wt_ref[:, TAIL_POOL:TAIL_POOL + G])
    halo_rows = halo_d.shape[0]
    extd = jnp.concatenate([halo_d[...], pd], axis=0)
    s2 = extd + _shift_rows(extd, 1)
    s4 = s2 + _shift_rows(s2, 2)
    s8 = s4 + _shift_rows(s4, 4)
    s16 = s8 + _shift_rows(s8, 8)
    lane = lax.broadcasted_iota(jnp.int32, (1, G), 1)
    wsum = jnp.where(lane < POOL_GROUP, s2,
                     jnp.where(lane < 2 * POOL_GROUP, s4,
                               jnp.where(lane < 3 * POOL_GROUP, s8, s16)))[halo_rows:, :]
    tpos = t * tm + lax.broadcasted_iota(jnp.int32, (tm, 1), 0)
    cnt = jnp.minimum(tpos + 1, pwin_ref[...]).astype(jnp.float32)
    z = (wsum / cnt - pd).astype(jnp.bfloat16)
    halo_d[...] = pd[tm - halo_rows:, :]

    pf = _dot(xn, wt_ref[:, TAIL_FORGET:TAIL_FORGET + LANES]) + bf_ref[...]
    c = (jnp.minimum(pf, 0.0) - jnp.log(1.0 + jnp.exp(-jnp.abs(pf)))) * LOG2_E
    row = lax.broadcasted_iota(jnp.int32, (tm, LANES), 0)
    k = 1
    while k < tm:
        c = c + jnp.where(row >= k, _shift_rows(c, k), 0.0)
        k *= 2
    c = c + carry_c[SUBLANES - 1:SUBLANES, :]
    c_ref[...] = c
    ct_ref[...] = c.T[0:SUBLANES, :]
    carry_c[...] = c[tm - SUBLANES:, :]

    cos2 = jnp.concatenate([cos_ref[...]] * (G // LANES), axis=1)
    sin2 = jnp.concatenate([sin_ref[...]] * (G // LANES), axis=1)
    first_half = (lane % HEAD_DIM) < (ROPE_DIM // 2)

    def rope(v):
        partner = jnp.where(first_half, pltpu.roll(v, G - ROPE_DIM // 2, 1),
                            pltpu.roll(v, ROPE_DIM // 2, 1))
        return v * cos2 + partner * sin2

    scale = HEAD_DIM ** -0.5 * LOG2_E
    pb = _dot(xn, w_ref[:, COL_DSWA:COL_DSWA + 3 * G])
    qkv_roped = (rope(pb[:, 0:G]) * scale, rope(pb[:, G:2 * G]), pb[:, 2 * G:3 * G])
    for part, val in enumerate(qkv_roped):
        for tile in range(G // LANES):
            qkvb_ref[part * (G // LANES) + tile] = val[:, tile * LANES:(tile + 1) * LANES]

    pc = _dot(xn, w_ref[:, COL_FOX:COL_FOX + 3 * G])
    qkc_ref[:, 0:G] = (pc[:, 0:G] * scale).astype(qkc_ref.dtype)
    qkc_ref[:, G:2 * G] = pc[:, G:2 * G].astype(qkc_ref.dtype)
    vtc_ref[...] = pc[:, 2 * G:3 * G].T.astype(vtc_ref.dtype)

    yd_ref[...] = (_dot(z, wpool_ref[...]) * pscale_ref[...]).astype(yd_ref.dtype)


def _mix_in(h, g, w_main, w_tail, cos_t, sin_t, wsc, wpool_bd, pscale, bf, pwin, seq, layer):
    n_tok = h.shape[0]
    tm = TOKEN_TILE
    G = GROUP_WIDTH
    row_spec = lambda width: pl.BlockSpec((tm, width), lambda i: (i, 0))
    slab_spec = lambda n: pl.BlockSpec((n, tm, LANES), lambda i: (0, i, 0))
    bf16 = jnp.bfloat16
    return pl.pallas_call(
        functools.partial(_mix_in_kernel, tiles_per_batch=seq // tm),
        grid=(n_tok // tm,),
        in_specs=[row_spec(D_MODEL), _layer_spec((1, D_MODEL), layer),
                  _layer_spec((D_MODEL, N_IN_MAIN), layer), _layer_spec((D_MODEL, N_IN_TAIL), layer),
                  row_spec(LANES), row_spec(LANES), _layer_spec((SUBLANES, G), layer),
                  _layer_spec((G, G), layer), _layer_spec((1, G), layer),
                  _layer_spec((1, LANES), layer), _const_spec((1, G))],
        out_specs=[row_spec(G), slab_spec(3 * G // LANES), row_spec(2 * G),
                   pl.BlockSpec((G, tm), lambda i: (0, i)), row_spec(G), row_spec(LANES),
                   pl.BlockSpec((SUBLANES, tm), lambda i: (0, i))],
        out_shape=[jax.ShapeDtypeStruct((n_tok, G), bf16),
                   jax.ShapeDtypeStruct((3 * G // LANES, n_tok, LANES), jnp.float32),
                   jax.ShapeDtypeStruct((n_tok, 2 * G), bf16),
                   jax.ShapeDtypeStruct((G, n_tok), bf16),
                   jax.ShapeDtypeStruct((n_tok, G), bf16),
                   jax.ShapeDtypeStruct((n_tok, LANES), jnp.float32),
                   jax.ShapeDtypeStruct((SUBLANES, n_tok), jnp.float32)],
        scratch_shapes=[pltpu.VMEM((SUBLANES, G), jnp.float32),
                        pltpu.VMEM((2 * SUBLANES, G), jnp.float32),
                        pltpu.VMEM((SUBLANES, LANES), jnp.float32)],
        compiler_params=_params(("arbitrary",)),
        name="mix_in",
    )(h, g, w_main, w_tail, cos_t, sin_t, wsc, wpool_bd, pscale, bf, pwin)


def _dswa_kernel(qkv_ref, out_ref, m_s, l_s, a_s):
    seq = qkv_ref.shape[1]
    QB = DSWA_BLOCK
    n_tiles = GROUP_WIDTH // LANES
    heads_per_tile = LANES // HEAD_DIM
    lane = lax.broadcasted_iota(jnp.int32, (1, LANES), 1)
    head_masks = [(lane // HEAD_DIM) == h for h in range(heads_per_tile)]
    qrow = lax.broadcasted_iota(jnp.int32, (QB, 2 * QB), 0)
    kcol = lax.broadcasted_iota(jnp.int32, (QB, 2 * QB), 1)
    band = (kcol >= qrow) & (kcol <= qrow + QB)
    ones_bd = jnp.concatenate(
        [jnp.broadcast_to(jnp.where(hm, 1.0, 0.0), (2 * QB, LANES)).astype(jnp.bfloat16)
         for hm in head_masks], axis=0)

    order = sorted(range(len(DSWA_CONFIGS)), key=lambda c: -DSWA_CONFIGS[c][1])
    for cfg in order:
        window, dil = DSWA_CONFIGS[cfg]
        assert window == QB * dil
        nblk = seq // (QB * dil)
        unroll = DSWA_UNROLL[cfg]
        assert nblk % unroll == 0 or unroll % nblk == 0
        run = min(unroll, nblk)
        runs_per_trip = unroll // run
        trips_per_residue = nblk // run

        def blocks(it, carry, first=(cfg == order[0]), dil=dil, run=run, runs_per_trip=runs_per_trip,
                   trips_per_residue=trips_per_residue):
            scored = []
            for r in range(runs_per_trip):
                if trips_per_residue == 1:
                    rho, i0 = it * runs_per_trip + r, 0
                else:
                    rho, i0 = it // trips_per_residue, (it % trips_per_residue) * run

                def rows_of(i, rho=rho):
                    start = rho + (QB * dil) * i
                    if dil > 1:
                        return pl.ds(start, QB, stride=dil)
                    return pl.ds(pl.multiple_of(start, QB), QB)

                first_has_prev = (i0 > 0) if trips_per_residue > 1 else False
                block_rows = [rows_of(i0 + u) for u in range(run)]
                prev_rows = rows_of(jnp.maximum(i0 - 1, 0)) if trips_per_residue > 1 else None

                for tile in range(n_tiles):
                    def key_value_tile(rows, tile=tile):
                        k = qkv_ref[n_tiles + tile, rows, :].astype(jnp.bfloat16)
                        v = qkv_ref[2 * n_tiles + tile, rows, :]
                        return k, [jnp.where(hm, v, 0.0).astype(jnp.bfloat16) for hm in head_masks]

                    tiles = [key_value_tile(rows) for rows in block_rows]
                    lead = key_value_tile(prev_rows) if prev_rows is not None else tiles[0]
                    for u in range(run):
                        (k_prev, v_prev), (k_cur, v_cur) = (lead if u == 0 else tiles[u - 1]), tiles[u]
                        valid = band if u > 0 else band & ((kcol >= QB) | first_has_prev)
                        q = qkv_ref[tile, block_rows[u], :]
                        kb = jnp.concatenate([k_prev, k_cur], axis=0)
                        s_heads = [_dot_nt(jnp.where(hm, q, 0.0).astype(jnp.bfloat16), kb)
                                   for hm in head_masks]
                        v_bd = jnp.concatenate(
                            [part for h in range(heads_per_tile) for part in (v_prev[h], v_cur[h])],
                            axis=0)
                        scored.append((block_rows[u], valid, tile, s_heads, v_bd))

            weighted = []
            for rows, valid, tile, s_heads, v_bd in scored:
                probs, maxes = [], []
                for s in s_heads:
                    s = jnp.where(valid, s, NEG_INF)
                    m = jnp.max(s, axis=1, keepdims=True)
                    probs.append(jnp.exp2(s - m).astype(jnp.bfloat16))
                    maxes.append(m)
                m_full = jnp.where(head_masks[0], maxes[0], maxes[1])
                weighted.append((rows, tile, jnp.concatenate(probs, axis=1), m_full,
                                 jnp.concatenate([v_bd, ones_bd], axis=1)))

            reduced = [(rows, tile, m_full, _dot(p_all, v_aug))
                       for rows, tile, p_all, m_full, v_aug in weighted]

            for rows, tile, m_full, acc_l in reduced:
                acc, l_full = acc_l[:, 0:LANES], acc_l[:, LANES:2 * LANES]
                if first:
                    m_s[tile, rows, :] = m_full
                    l_s[tile, rows, :] = l_full
                    a_s[tile, rows, :] = acc
                else:
                    m_old = m_s[tile, rows, :]
                    m_new = jnp.maximum(m_old, m_full)
                    w_old = jnp.exp2(m_old - m_new)
                    w_cur = jnp.exp2(m_full - m_new)
                    m_s[tile, rows, :] = m_new
                    l_s[tile, rows, :] = l_s[tile, rows, :] * w_old + l_full * w_cur
                    a_s[tile, rows, :] = a_s[tile, rows, :] * w_old + acc * w_cur
            return carry

        lax.fori_loop(0, seq // (QB * unroll), blocks, 0)

    chunk = TOKEN_TILE
    for tile in range(n_tiles):
        for r in range(0, seq, chunk):
            out_ref[0, r:r + chunk, tile * LANES:(tile + 1) * LANES] = (
                a_s[tile, r:r + chunk, :] / l_s[tile, r:r + chunk, :]).astype(out_ref.dtype)


def _dswa(qkvb, batch, seq):
    G = GROUP_WIDTH
    n_slabs = 3 * G // LANES
    return pl.pallas_call(
        _dswa_kernel,
        grid=(batch,),
        in_specs=[pl.BlockSpec((n_slabs, seq, LANES), lambda b: (0, b, 0))],
        out_specs=pl.BlockSpec((1, seq, G), lambda b: (b, 0, 0)),
        out_shape=jax.ShapeDtypeStruct((batch, seq, G), jnp.bfloat16),
        scratch_shapes=[pltpu.VMEM((G // LANES, seq, LANES), jnp.float32)] * 3,
        compiler_params=_params(("arbitrary",)),
        name="dswa",
    )(qkvb)


def _fox_kernel(qk_ref, vt_ref, c_ref, ct_ref, out_ref):
    G = GROUP_WIDTH
    BQ = FOX_BLOCK
    H = N_GROUP_HEADS
    seq = qk_ref.shape[1]
    lane = lax.broadcasted_iota(jnp.int32, (1, G), 1)
    krow = lax.broadcasted_iota(jnp.int32, (BQ, BQ), 0)
    qcol = lax.broadcasted_iota(jnp.int32, (BQ, BQ), 1)
    causal = krow <= qcol
    ones_rows = jnp.ones((BF16_SUBLANES, BQ), jnp.bfloat16)

    def query_block(qi, _):
        q0 = pl.multiple_of(qi * BQ, BQ)
        q = qk_ref[0, pl.ds(q0, BQ), 0:G].astype(jnp.float32)
        q_heads = [jnp.where((lane // HEAD_DIM) == h, q, 0.0).astype(jnp.bfloat16) for h in range(H)]
        c_q = [ct_ref[h:h + 1, pl.ds(q0, BQ)] for h in range(H)]
        _fox_query_block(qi, q0, q_heads, c_q, causal, ones_rows, qk_ref, vt_ref, c_ref, out_ref)
        return 0

    lax.fori_loop(0, seq // BQ, query_block, 0)


def _fox_query_block(qi, q0, q_heads, c_q, causal, ones_rows, qk_ref, vt_ref, c_ref, out_ref):
    G = GROUP_WIDTH
    BQ = FOX_BLOCK
    H = N_GROUP_HEADS

    def kv_step(blocks, carry):
        starts = [pl.multiple_of(j * BQ, BQ) for j, _ in blocks]
        keys = [qk_ref[0, pl.ds(k0, BQ), G:2 * G] for k0 in starts]
        c_keys = [c_ref[0, pl.ds(k0, BQ), :] for k0 in starts]
        chains = [(b, h) for b in range(len(blocks)) for h in range(H)]

        def scores(b, h):
            return _dot_nt(keys[b], q_heads[h]) - c_keys[b][:, h:h + 1]

        pending = [scores(*chains[n]) for n in range(min(FOX_LOOKAHEAD, len(chains)))]
        state = list(carry)
        for n, (b, h) in enumerate(chains):
            m, acc = state[h]
            s = pending.pop(0)
            if n + FOX_LOOKAHEAD < len(chains):
                pending.append(scores(*chains[n + FOX_LOOKAHEAD]))
            if blocks[b][1]:
                s = jnp.where(causal, s, NEG_INF)
            m_new = jnp.maximum(m, jnp.max(s, axis=0, keepdims=True) + c_q[h])
            p = jnp.exp2(s + (c_q[h] - m_new)).astype(jnp.bfloat16)
            v_t = jnp.concatenate(
                [vt_ref[h * HEAD_DIM:(h + 1) * HEAD_DIM, pl.ds(starts[b], BQ)], ones_rows], axis=0)
            state[h] = (m_new, jnp.exp2(m - m_new) * acc + _dot(v_t, p))
        return tuple(state)

    init = tuple((jnp.full((1, BQ), NEG_INF, jnp.float32),
                  jnp.zeros((HEAD_DIM + BF16_SUBLANES, BQ), jnp.float32)) for _ in range(H))
    carry = lax.fori_loop(
        0, qi // 2, lambda t, c: kv_step([(2 * t, False), (2 * t + 1, False)], c), init)
    final = lax.cond(qi % 2 == 1,
                     lambda c: kv_step([(qi - 1, False), (qi, True)], c),
                     lambda c: kv_step([(qi, True)], c), carry)
    out_t = jnp.concatenate([acc[0:HEAD_DIM] / acc[HEAD_DIM:HEAD_DIM + 1] for _, acc in final],
                            axis=0)
    out_ref[0, pl.ds(q0, BQ), :] = out_t.T.astype(out_ref.dtype)


def _fox(qk, vt, c, ct, batch, seq):
    G = GROUP_WIDTH
    return pl.pallas_call(
        _fox_kernel,
        grid=(batch,),
        in_specs=[pl.BlockSpec((1, seq, 2 * G), lambda b: (b, 0, 0)),
                  pl.BlockSpec((G, seq), lambda b: (0, b)),
                  pl.BlockSpec((1, seq, LANES), lambda b: (b, 0, 0)),
                  pl.BlockSpec((SUBLANES, seq), lambda b: (0, b))],
        out_specs=pl.BlockSpec((1, seq, G), lambda b: (b, 0, 0)),
        out_shape=jax.ShapeDtypeStruct((batch, seq, G), jnp.bfloat16),
        compiler_params=_params(("arbitrary",)),
        name="fox",
    )(qk.reshape(batch, seq, 2 * G), vt, c.reshape(batch, seq, LANES), ct)


def _mem_kv_kernel(mem_ref, g_ref, w_ref, kv_ref):
    memn = _rms_norm(mem_ref[0], g_ref[...]).astype(jnp.bfloat16)
    kv = _dot(memn, w_ref[...])
    kv_ref[0, :, 0:D_MODEL] = (kv[:, 0:D_MODEL] * (XA_HEAD_DIM ** -0.5 * LOG2_E)).astype(kv_ref.dtype)
    kv_ref[0, :, D_MODEL:] = kv[:, D_MODEL:].astype(kv_ref.dtype)


def _mem_kv(mem, g, w_xkv, layer):
    batch, mem_len, _ = mem.shape
    return pl.pallas_call(
        _mem_kv_kernel,
        grid=(batch,),
        in_specs=[pl.BlockSpec((1, mem_len, D_MODEL), lambda b: (b, 0, 0)),
                  _layer_spec((1, D_MODEL), layer), _layer_spec((D_MODEL, 2 * D_MODEL), layer)],
        out_specs=pl.BlockSpec((1, mem_len, 2 * D_MODEL), lambda b: (b, 0, 0)),
        out_shape=jax.ShapeDtypeStruct((batch, mem_len, 2 * D_MODEL), jnp.bfloat16),
        compiler_params=_params(("arbitrary",)),
        name="mem_kv",
    )(mem, g, w_xkv)


def _mix_out_kernel(h_ref, ya_ref, yb_ref, yc_ref, yd_ref, wout_ref, g_ref, wq_ref, kv_ref, wo_ref,
                    out_ref):
    y = jnp.concatenate([ya_ref[...], yb_ref[...], yc_ref[...], yd_ref[...]], axis=1)
    h1 = h_ref[...] + _dot(y, wout_ref[...])
    q = _dot(_rms_norm(h1, g_ref[...]).astype(jnp.bfloat16), wq_ref[...])
    head_cols = [slice(hh * XA_HEAD_DIM, (hh + 1) * XA_HEAD_DIM) for hh in range(XA_HEADS)]
    scores = [_dot_nt(q[:, cols].astype(jnp.bfloat16), kv_ref[0, :, cols]) for cols in head_cols]
    probs = [jnp.exp2(s - jnp.max(s, axis=1, keepdims=True)) for s in scores]
    heads = []
    for cols, p in zip(head_cols, probs):
        o = _dot(p.astype(jnp.bfloat16), kv_ref[0, :, D_MODEL + cols.start:D_MODEL + cols.stop])
        heads.append((o / jnp.sum(p, axis=1, keepdims=True)).astype(jnp.bfloat16))
    out_ref[...] = h1 + _dot(jnp.concatenate(heads, axis=1), wo_ref[...])


def _mix_out(h, ya, yb, yc, yd, w_out, g_xa, w_xq, kv, w_xo, seq, layer):
    n_tok = h.shape[0]
    tm = TOKEN_TILE
    G = GROUP_WIDTH
    mem_len = kv.shape[1]
    tiles_per_batch = seq // tm
    row_spec = lambda width: pl.BlockSpec((tm, width), lambda i: (i, 0))
    return pl.pallas_call(
        _mix_out_kernel,
        grid=(n_tok // tm,),
        in_specs=[row_spec(D_MODEL), row_spec(G), row_spec(G), row_spec(G), row_spec(G),
                  _layer_spec((D_MODEL, D_MODEL), layer), _layer_spec((1, D_MODEL), layer),
                  _layer_spec((D_MODEL, D_MODEL), layer),
                  pl.BlockSpec((1, mem_len, 2 * D_MODEL), lambda i: (i // tiles_per_batch, 0, 0)),
                  _layer_spec((D_MODEL, D_MODEL), layer)],
        out_specs=row_spec(D_MODEL),
        out_shape=jax.ShapeDtypeStruct((n_tok, D_MODEL), jnp.float32),
        compiler_params=_params(("arbitrary",)),
        name="mix_out",
    )(h, ya, yb, yc, yd, w_out, g_xa, w_xq, kv, w_xo)


def _ffn_kernel(h_ref, g_ref, wup_ref, wconv_ref, wdown_ref, gfin_ref, out_ref, act_s, halo_s,
                *, tiles_per_batch, final_norm):
    tm = h_ref.shape[0]
    t = pl.program_id(0) % tiles_per_batch

    @pl.when(t == 0)
    def _():
        halo_s[...] = jnp.zeros_like(halo_s)

    h = h_ref[...]
    xn = _rms_norm(h, g_ref[...]).astype(jnp.bfloat16)

    def conv_branch(col):
        cols = slice(col, col + FFN_CHUNK)
        u = _dot(xn, wup_ref[:, cols])
        ext = jnp.concatenate([halo_s[:, cols], u], axis=0)
        y = (wconv_ref[2:3, cols] * ext + wconv_ref[1:2, cols] * _shift_rows(ext, 1)
             + wconv_ref[0:1, cols] * _shift_rows(ext, 2))
        halo_s[:, cols] = u[tm - SUBLANES:, :]
        return y[SUBLANES:, :]

    for col in range(0, D_FF, FFN_CHUNK):
        a = conv_branch(col)
        gate = conv_branch(D_FF + col)
        act_s[:, col:col + FFN_CHUNK] = (a * (gate * jax.nn.sigmoid(gate))).astype(act_s.dtype)

    out = h + _dot(act_s[...], wdown_ref[...])
    if final_norm:
        out = _rms_norm(out, gfin_ref[...])
    out_ref[...] = out


def _ffn(h, g_ffn, w_up, w_conv, w_down, g_final, seq, layer, final_norm):
    n_tok = h.shape[0]
    tm = TOKEN_TILE
    row_spec = pl.BlockSpec((tm, D_MODEL), lambda i: (i, 0))
    return pl.pallas_call(
        functools.partial(_ffn_kernel, tiles_per_batch=seq // tm, final_norm=final_norm),
        grid=(n_tok // tm,),
        in_specs=[row_spec, _layer_spec((1, D_MODEL), layer), _layer_spec((D_MODEL, 2 * D_FF), layer),
                  _layer_spec((SUBLANES, 2 * D_FF), layer), _layer_spec((D_FF, D_MODEL), layer),
                  _const_spec((1, D_MODEL))],
        out_specs=row_spec,
        out_shape=jax.ShapeDtypeStruct((n_tok, D_MODEL), jnp.float32),
        scratch_shapes=[pltpu.VMEM((tm, D_FF), jnp.bfloat16),
                        pltpu.VMEM((SUBLANES, 2 * D_FF), jnp.float32)],
        compiler_params=_params(("arbitrary",)),
        name="ffn",
    )(h, g_ffn, w_up, w_conv, w_down, g_final)


def _pad_axis(w, size, axis):
    pad = [(0, 0)] * w.ndim
    pad[axis] = (0, size - w.shape[axis])
    return jnp.pad(w, pad)


def kernel(x, mem, positions, g_mix, w_in, b_forget, w_sconv, w_pool, pool_scale, w_out, g_xa, g_mem,
           w_xq, w_xkv, w_xo, g_ffn, w_up, w_ffconv, w_down, g_final):
    batch, seq, _ = x.shape
    depth = w_in.shape[0]
    G = GROUP_WIDTH
    bf16 = jnp.bfloat16
    n_tok = batch * seq

    row = lambda p: p.reshape(depth, 1, p.shape[-1])
    w_main = _pad_axis(w_in, pl.cdiv(w_in.shape[2], LANES) * LANES, 2).astype(bf16)
    w_tail = _pad_axis(jnp.concatenate([w_in[:, :, N_IN_MAIN + N_GROUP_HEADS:],
                                        w_in[:, :, N_IN_MAIN:N_IN_MAIN + N_GROUP_HEADS]], axis=2),
                       N_IN_TAIL, 2).astype(bf16)
    bf = row(_pad_axis(b_forget, LANES, 1))
    n_groups = len(POOL_WINDOWS)
    wpool_bd = (w_pool[:, :, :, None, :] * jnp.eye(n_groups, dtype=w_pool.dtype)[None, :, None, :, None]
                ).reshape(depth, G, G).astype(bf16)
    wsc = _pad_axis(w_sconv, SUBLANES, 1)
    wfc = _pad_axis(w_ffconv, SUBLANES, 1)
    w_out_b, w_xq_b, w_xkv_b, w_xo_b = (w.astype(bf16) for w in (w_out, w_xq, w_xkv, w_xo))
    w_up_b, w_down_b = w_up.astype(bf16), w_down.astype(bf16)
    pwin = jnp.repeat(jnp.asarray(POOL_WINDOWS, jnp.int32), POOL_GROUP).reshape(1, G)

    cos_t, sin_t = _rope_tables(positions)
    h = x.reshape(n_tok, D_MODEL)
    for l in range(depth):
        ya, qkvb, qkc, vtc, yd, c, ct = _mix_in(
            h, row(g_mix), w_main, w_tail, cos_t, sin_t, wsc, wpool_bd, row(pool_scale), bf, pwin, seq, l)
        yb = _dswa(qkvb, batch, seq).reshape(n_tok, G)
        yc = _fox(qkc, vtc, c, ct, batch, seq).reshape(n_tok, G)
        kv = _mem_kv(mem, row(g_mem), w_xkv_b, l)
        h = _mix_out(h, ya, yb, yc, yd, w_out_b, row(g_xa), w_xq_b, kv, w_xo_b, seq, l)
        h = _ffn(h, row(g_ffn), w_up_b, wfc, w_down_b, g_final.reshape(1, D_MODEL), seq, l,
                 final_norm=(l == depth - 1))
    return h.reshape(batch, seq, D_MODEL)
```
